```python
import math
import jax, jax.numpy as jnp
from jax import lax
import numpy as np

D_MODEL = 2048
BATCH = 2
SEQ = 8192
DEPTH = 2
DEC_BATCH = 16
DEC_SEQ = 64
PAST_LEN = 4096

CHUNK = 64
Q_BLOCK = 128
HEAD_DIM = 128
N_HEADS_A = 8
N_HEADS_B = 8
DK_B = 128
DV_B = 128
WIDTH_A = N_HEADS_A * HEAD_DIM
WIDTH_B = N_HEADS_B * DV_B
PROJ_WIDTH = 3 * WIDTH_A + N_HEADS_A + 2 * N_HEADS_B * DK_B + 2 * WIDTH_B
CONV_WIDTH = 31
CONV_CH = D_MODEL
D_FF = -(-8 * D_MODEL // 768) * 256
N_ATT_LAYERS = (DEPTH + 1) // 2
N_CONV_LAYERS = DEPTH // 2
ROPE_BASE = 10000.0
EPS = 1e-6
NEG = -1e30

kernel_name = "fox_retention_conformer_stream_step"


def _rms(x, g):
    xf = x.astype(jnp.float32)
    y = xf * lax.rsqrt(jnp.mean(jnp.square(xf), axis=-1, keepdims=True) + EPS)
    return (y * g.astype(jnp.float32)).astype(x.dtype)


def _ln(xf):
    mu = jnp.mean(xf, axis=-1, keepdims=True)
    xc = xf - mu
    return xc * lax.rsqrt(jnp.mean(jnp.square(xc), axis=-1, keepdims=True) + EPS)


def _rope(x, pos):
    half = x.shape[-1] // 2
    inv = jnp.exp(-math.log(ROPE_BASE) * jnp.arange(half, dtype=jnp.float32) / half)
    ang = pos.astype(jnp.float32)[:, None] * inv[None, :]
    cos = jnp.cos(ang)[None, :, None, :]
    sin = jnp.sin(ang)[None, :, None, :]
    xf = x.astype(jnp.float32)
    x1, x2 = xf[..., :half], xf[..., half:]
    return jnp.concatenate([x1 * cos - x2 * sin, x1 * sin + x2 * cos], axis=-1).astype(x.dtype)


def _fox_block(q, k, v, cq, ck, qpos, kpos):
    s = jnp.einsum('bqhd,bkhd->bhqk', q, k, preferred_element_type=jnp.float32) * (HEAD_DIM ** -0.5)
    bias = jnp.transpose(cq, (0, 2, 1))[:, :, :, None] - jnp.transpose(ck, (0, 2, 1))[:, :, None, :]
    mask = kpos[None, :] <= qpos[:, None]
    p = jax.nn.softmax(jnp.where(mask[None, None], s + bias, NEG), axis=-1)
    return jnp.einsum('bhqk,bkhd->bqhd', p.astype(v.dtype), v)


def _fox_prompt(q, k, v, c, pos):
    B, S, H, d = q.shape
    nb = S // Q_BLOCK
    qb = q.reshape(B, nb, Q_BLOCK, H, d).transpose(1, 0, 2, 3, 4)
    cb = c.reshape(B, nb, Q_BLOCK, H).transpose(1, 0, 2, 3)
    pb = pos.reshape(nb, Q_BLOCK)
    out = lax.map(lambda a: _fox_block(a[0], k, v, a[1], c, a[2], pos), (qb, cb, pb))
    return out.transpose(1, 0, 2, 3, 4).reshape(B, S, H, d)


def _retention(q, k, v, s0):
    B, T, H, dk = q.shape
    dv = v.shape[-1]
    L = min(CHUNK, T)
    nc = T // L
    log_g = jnp.log1p(-jnp.exp2(-5.0 - jnp.arange(H, dtype=jnp.float32)))
    i = jnp.arange(L, dtype=jnp.float32)
    diff = i[:, None] - i[None, :]
    dmat = jnp.where(diff[None] >= 0, jnp.exp(jnp.maximum(diff, 0.0)[None] * log_g[:, None, None]), 0.0)
    q_dec = jnp.exp((i + 1.0)[None, :] * log_g[:, None])[None, :, :, None]
    k_dec = jnp.exp((L - 1.0 - i)[None, :] * log_g[:, None])[None, :, :, None]
    c_dec = jnp.exp(L * log_g)[None, :, None, None]
    def blocks(a):
        return a.astype(jnp.float32).reshape(B, nc, L, H, a.shape[-1]).transpose(1, 0, 3, 2, 4)
    def step(s, inp):
        qi, ki, vi = inp
        att = jnp.einsum('bhld,bhmd->bhlm', qi, ki) * dmat[None]
        o = jnp.einsum('bhlm,bhmv->bhlv', att, vi) + jnp.einsum('bhld,bhdv->bhlv', qi * q_dec, s)
        s = s * c_dec + jnp.einsum('bhld,bhlv->bhdv', ki * k_dec, vi)
        return s, o
    s, o = lax.scan(step, s0.astype(jnp.float32), (blocks(q), blocks(k), blocks(v)))
    return o.transpose(1, 0, 3, 2, 4).reshape(B, T, H, dv), s


def _att_ret_mixer(h, w_in, b_f, gq, gk, g_gn, w_out, pos, past):
    B, T, _ = h.shape
    proj = h @ w_in
    sizes = (WIDTH_A, WIDTH_A, WIDTH_A, N_HEADS_A, N_HEADS_B * DK_B, N_HEADS_B * DK_B, WIDTH_B)
    splits = [int(s) for s in np.cumsum(sizes)]
    qa, ka, va, fa, qb, kb, vb, gb = jnp.split(proj, splits, axis=-1)
    qa = _rms(qa.reshape(B, T, N_HEADS_A, HEAD_DIM), gq)
    ka = _rms(ka.reshape(B, T, N_HEADS_A, HEAD_DIM), gk)
    va = va.reshape(B, T, N_HEADS_A, HEAD_DIM)
    logf = jax.nn.log_sigmoid((fa + b_f).astype(jnp.float32))
    qb = _rope(qb.reshape(B, T, N_HEADS_B, DK_B), pos)
    kb = _rope(kb.reshape(B, T, N_HEADS_B, DK_B), pos) * (DK_B ** -0.5)
    vb = vb.reshape(B, T, N_HEADS_B, DV_B)
    if past is None:
        c = jnp.cumsum(logf, axis=1)
        oa = _fox_prompt(qa, ka, va, c, pos)
        s0 = jnp.zeros((B, N_HEADS_B, DK_B, DV_B), jnp.float32)
    else:
        kc, vc, lfc, s0 = past
        k_all = jnp.concatenate([kc.astype(ka.dtype), ka], axis=1)
        v_all = jnp.concatenate([vc.astype(va.dtype), va], axis=1)
        c = jnp.cumsum(jnp.concatenate([lfc.astype(jnp.float32), logf], axis=1), axis=1)
        kpos = jnp.arange(kc.shape[1] + T)
        oa = _fox_block(qa, k_all, v_all, c[:, -T:], c, pos, kpos)
    ob, s_new = _retention(qb, kb, vb, s0)
    ob = _ln(ob).reshape(B, T, WIDTH_B) * g_gn.astype(jnp.float32) * jax.nn.silu(gb.astype(jnp.float32))
    mixed = jnp.concatenate([oa.reshape(B, T, WIDTH_A).astype(h.dtype), ob.astype(h.dtype)], axis=-1)
    return mixed @ w_out, (ka, va, logf, s_new)


def _conv_mixer(h, w1, b1, w_dw, b_dw, ln_g, ln_b, w2, b2, buf):
    B, T, _ = h.shape
    a, g = jnp.split(h @ w1 + b1, 2, axis=-1)
    glu = a * jax.nn.sigmoid(g)
    if buf is None:
        buf = jnp.zeros((B, CONV_WIDTH - 1, CONV_CH), glu.dtype)
    xp = jnp.concatenate([buf.astype(glu.dtype), glu], axis=1)
    z = lax.conv_general_dilated(xp, w_dw[:, None, :].astype(xp.dtype), window_strides=(1,), padding='VALID',
                                 dimension_numbers=('NWC', 'WIO', 'NWC'), feature_group_count=CONV_CH) + b_dw
    z = _ln(z.astype(jnp.float32)) * ln_g.astype(jnp.float32) + ln_b.astype(jnp.float32)
    z = jax.nn.silu(z).astype(h.dtype)
    return z @ w2 + b2, xp[:, -(CONV_WIDTH - 1):]


def _swiglu(h, wg, wu, wd):
    return (jax.nn.silu(h @ wg) * (h @ wu)) @ wd


def setup_inputs(seed: int = 0) -> dict:
    key = jax.random.key(seed)
    ks = iter(jax.random.split(key, 32))
    def nrm(shape, scale):
        return scale * jax.random.normal(next(ks), shape, jnp.float32)
    NA, NC = N_ATT_LAYERS, N_CONV_LAYERS
    return {
        "x_prompt": nrm((BATCH, SEQ, D_MODEL), 1.0),
        "x_sample": nrm((DEC_BATCH, DEC_SEQ, D_MODEL), 1.0),
        "cache_k": nrm((NA, DEC_BATCH, PAST_LEN, N_HEADS_A, HEAD_DIM), 1.0),
        "cache_v": nrm((NA, DEC_BATCH, PAST_LEN, N_HEADS_A, HEAD_DIM), 1.0),
        "cache_logf": jax.nn.log_sigmoid(3.0 + nrm((NA, DEC_BATCH, PAST_LEN, N_HEADS_A), 1.0)),
        "state_ret": nrm((NA, DEC_BATCH, N_HEADS_B, DK_B, DV_B), 0.3),
        "state_conv": nrm((NC, DEC_BATCH, CONV_WIDTH - 1, CONV_CH), 0.5),
        "norm_mix": 1.0 + nrm((DEPTH, D_MODEL), 0.02),
        "norm_ffn": 1.0 + nrm((DEPTH, D_MODEL), 0.02),
        "w_in": nrm((NA, D_MODEL, PROJ_WIDTH), D_MODEL ** -0.5),
        "b_forget": jnp.linspace(1.0, 6.0, N_HEADS_A, dtype=jnp.float32)[None, :] + nrm((NA, N_HEADS_A), 0.1),
        "q_norm_gain": 1.0 + nrm((NA, HEAD_DIM), 0.02),
        "k_norm_gain": 1.0 + nrm((NA, HEAD_DIM), 0.02),
        "ret_norm_gain": 1.0 + nrm((NA, WIDTH_B), 0.02),
        "w_out": nrm((NA, WIDTH_A + WIDTH_B, D_MODEL), (WIDTH_A + WIDTH_B) ** -0.5),
        "w_pw1": nrm((NC, D_MODEL, 2 * CONV_CH), D_MODEL ** -0.5),
        "b_pw1": nrm((NC, 2 * CONV_CH), 0.02),
        "w_dw": nrm((NC, CONV_WIDTH, CONV_CH), CONV_WIDTH ** -0.5),
        "b_dw": nrm((NC, CONV_CH), 0.02),
        "conv_ln_gain": 1.0 + nrm((NC, CONV_CH), 0.02),
        "conv_ln_bias": nrm((NC, CONV_CH), 0.02),
        "w_pw2": nrm((NC, CONV_CH, D_MODEL), CONV_CH ** -0.5),
        "b_pw2": nrm((NC, D_MODEL), 0.02),
        "w_ffn_gate": nrm((DEPTH, D_MODEL, D_FF), D_MODEL ** -0.5),
        "w_ffn_up": nrm((DEPTH, D_MODEL, D_FF), D_MODEL ** -0.5),
        "w_ffn_down": nrm((DEPTH, D_FF, D_MODEL), D_FF ** -0.5),
    }


def reference(x_prompt, x_sample, cache_k, cache_v, cache_logf, state_ret, state_conv,
              norm_mix, norm_ffn, w_in, b_forget, q_norm_gain, k_norm_gain, ret_norm_gain, w_out,
              w_pw1, b_pw1, w_dw, b_dw, conv_ln_gain, conv_ln_bias, w_pw2, b_pw2,
              w_ffn_gate, w_ffn_up, w_ffn_down):
    def trunk(x, pos, att_past, conv_past):
        k_l, v_l, lf_l, r_l, c_l = [], [], [], [], []
        for layer in range(DEPTH):
            j = layer // 2
            h = _rms(x, norm_mix[layer])
            if layer % 2 == 0:
                past = None if att_past is None else tuple(a[j] for a in att_past)
                y, (k_new, v_new, lf_new, r_new) = _att_ret_mixer(
                    h, w_in[j], b_forget[j], q_norm_gain[j], k_norm_gain[j], ret_norm_gain[j], w_out[j], pos, past)
                k_l.append(k_new); v_l.append(v_new); lf_l.append(lf_new); r_l.append(r_new)
            else:
                buf = None if conv_past is None else conv_past[j]
                y, c_new = _conv_mixer(h, w_pw1[j], b_pw1[j], w_dw[j], b_dw[j], conv_ln_gain[j], conv_ln_bias[j],
                                       w_pw2[j], b_pw2[j], buf)
                c_l.append(c_new)
            x = x + y
            x = x + _swiglu(_rms(x, norm_ffn[layer]), w_ffn_gate[layer], w_ffn_up[layer], w_ffn_down[layer])
        return x, jnp.stack(k_l), jnp.stack(v_l), jnp.stack(lf_l), jnp.stack(r_l), jnp.stack(c_l)

    pos_p = jnp.arange(x_prompt.shape[1])
    y_prompt, k_p, v_p, lf_p, r_p, c_p = trunk(x_prompt, pos_p, None, None)
    pos_s = cache_k.shape[2] + jnp.arange(x_sample.shape[1])
    y_sample, k_s, v_s, lf_s, r_s, c_s = trunk(x_sample, pos_s, (cache_k, cache_v, cache_logf, state_ret), state_conv)
    return (y_prompt, y_sample, k_p, v_p, lf_p, r_p, c_p, k_s, v_s, lf_s, r_s, c_s)
```

```python
import functools
import math

import jax
import jax.numpy as jnp
import numpy as np
from jax import lax
from jax.experimental import pallas as pl
from jax.experimental.pallas import tpu as pltpu

F32 = jnp.float32
BF16 = jnp.bfloat16

LANES = 128
HALO = 32
VMEM_LIMIT = 56 * 1024 * 1024

HEAD_DIM = 128
N_HEADS = 8
WIDTH = N_HEADS * HEAD_DIM
CONV_WIDTH = 31
ROPE_BASE = 10000.0
EPS = 1e-6
NEG = -1e30
RET_CHUNK = 256


def _params(n_grid):
    return pltpu.CompilerParams(dimension_semantics=("arbitrary",) * n_grid, vmem_limit_bytes=VMEM_LIMIT)


def _silu(x):
    return x * jax.nn.sigmoid(x)


def _dot(a, b):
    return jnp.dot(a, b, preferred_element_type=F32)


def _dot_nt(a, b):
    return lax.dot_general(a, b, (((1,), (1,)), ((), ())), preferred_element_type=F32)


def _rms_kernel(x_ref, g_ref, o_ref):
    x = x_ref[...]
    ms = jnp.mean(x * x, axis=-1, keepdims=True)
    o_ref[...] = (x * lax.rsqrt(ms + EPS) * g_ref[...]).astype(o_ref.dtype)


def _rmsnorm(x, g, name, tm=512):
    M, D = x.shape
    return pl.pallas_call(
        _rms_kernel,
        grid=(M // tm,),
        in_specs=[pl.BlockSpec((tm, D), lambda i: (i, 0)), pl.BlockSpec((1, D), lambda i: (0, 0))],
        out_specs=pl.BlockSpec((tm, D), lambda i: (i, 0)),
        out_shape=jax.ShapeDtypeStruct((M, D), BF16),
        compiler_params=_params(1),
        name=name,
    )(x, g.reshape(1, D))


def _mm_kernel(*refs, n_lhs, n_rhs, n_extra, products, epilogue):
    lhs = refs[:n_lhs]
    rhs = refs[n_lhs:n_lhs + n_rhs]
    extras = refs[n_lhs + n_rhs:n_lhs + n_rhs + n_extra]
    outs = refs[n_lhs + n_rhs + n_extra:]
    accs = []
    for prod in products:
        acc = None
        for a, b in prod:
            d = _dot(lhs[a][...], rhs[b][...])
            acc = d if acc is None else acc + d
        accs.append(acc)
    epilogue(accs, extras, outs)


def _matmul(name, lhs, rhs, products, epilogue, out_dtypes, *, tile=(), col=(), row=(), full=(), tm=512, tn=512):
    M = lhs[0].shape[0]
    N = rhs[0].shape[1]
    assert M % tm == 0 and N % tn == 0
    in_specs = [pl.BlockSpec((tm, a.shape[1]), lambda j, i: (i, 0)) for a in lhs]
    in_specs += [pl.BlockSpec((b.shape[0], tn), lambda j, i: (0, j)) for b in rhs]
    in_specs += [pl.BlockSpec((tm, tn), lambda j, i: (i, j)) for _ in tile]
    in_specs += [pl.BlockSpec((1, tn), lambda j, i: (0, j)) for _ in col]
    for r in row:
        nblk = r.shape[0] // tm
        assert r.shape[0] % tm == 0
        in_specs.append(pl.BlockSpec((tm, r.shape[1]), lambda j, i, nblk=nblk: (i % nblk, 0)))
    in_specs += [pl.BlockSpec(f.shape, lambda j, i, nd=f.ndim: (0,) * nd) for f in full]
    extras = tuple(tile) + tuple(col) + tuple(row) + tuple(full)
    kern = functools.partial(_mm_kernel, n_lhs=len(lhs), n_rhs=len(rhs), n_extra=len(extras),
                             products=products, epilogue=epilogue)
    res = pl.pallas_call(
        kern,
        grid=(N // tn, M // tm),
        in_specs=in_specs,
        out_specs=[pl.BlockSpec((tm, tn), lambda j, i: (i, j)) for _ in out_dtypes],
        out_shape=[jax.ShapeDtypeStruct((M, N), dt) for dt in out_dtypes],
        compiler_params=_params(2),
        name=name,
    )(*lhs, *rhs, *extras)
    return res


def _ep_headnorm(accs, extras, outs):
    gain = extras[0][...]
    y = accs[0]
    for h in range(y.shape[1] // HEAD_DIM):
        sl = slice(h * HEAD_DIM, (h + 1) * HEAD_DIM)
        yh = y[:, sl]
        ms = jnp.mean(yh * yh, axis=-1, keepdims=True)
        r = yh * lax.rsqrt(ms + EPS) * gain
        for o in outs:
            o[:, sl] = r.astype(o.dtype)


def _ep_identity(accs, extras, outs):
    for o in outs:
        o[...] = accs[0].astype(o.dtype)


def _ep_logsigmoid(accs, extras, outs):
    z = accs[0] + extras[0][...]
    outs[0][...] = -(jnp.maximum(-z, 0.0) + jnp.log1p(jnp.exp(-jnp.abs(z))))


def _ep_rope(accs, extras, outs):
    scale, cos, sin = extras[0][...], extras[1][...], extras[2][...]
    y = accs[0]
    for h in range(y.shape[1] // HEAD_DIM):
        sl = slice(h * HEAD_DIM, (h + 1) * HEAD_DIM)
        yh = y[:, sl]
        r = yh * cos + pltpu.roll(yh, HEAD_DIM // 2, 1) * sin
        outs[0][:, sl] = (r * scale[:, sl]).astype(outs[0].dtype)


def _ep_residual(accs, extras, outs):
    outs[0][...] = extras[0][...] + accs[0]


def _ep_bias_residual(accs, extras, outs):
    outs[0][...] = extras[0][...] + (accs[0] + extras[1][...])


def _ep_swiglu(accs, extras, outs):
    outs[0][...] = (_silu(accs[0]) * accs[1]).astype(outs[0].dtype)


def _ep_glu(accs, extras, outs):
    a = accs[0] + extras[0][...]
    g = accs[1] + extras[1][...]
    outs[0][...] = a * jax.nn.sigmoid(g)


def _cumsum_kernel(x_ref, o_ref):
    nblk, R, _ = x_ref.shape
    row = lax.broadcasted_iota(jnp.int32, (LANES, LANES), 0)
    col = lax.broadcasted_iota(jnp.int32, (LANES, LANES), 1)
    tri = jnp.where(row <= col, 1.0, 0.0).astype(BF16)
    ones = jnp.ones((LANES, LANES), BF16)

    def body(b, carry):
        x = x_ref[b]
        hi = x.astype(BF16)
        r1 = x - hi.astype(F32)
        mid = r1.astype(BF16)
        lo = (r1 - mid.astype(F32)).astype(BF16)
        within = _dot(hi, tri) + _dot(mid, tri) + _dot(lo, tri)
        total = _dot(hi, ones) + _dot(mid, ones) + _dot(lo, ones)
        o_ref[b] = carry + within
        return carry + total

    lax.fori_loop(0, nblk, body, jnp.zeros((R, LANES), F32))


def _cumsum_rows(x, name):
    R, N = x.shape
    nblk = N // LANES
    x3 = x.reshape(R, nblk, LANES).transpose(1, 0, 2)
    c3 = pl.pallas_call(
        _cumsum_kernel,
        out_shape=jax.ShapeDtypeStruct((nblk, R, LANES), F32),
        compiler_params=pltpu.CompilerParams(vmem_limit_bytes=VMEM_LIMIT),
        name=name,
    )(x3)
    return c3.transpose(1, 0, 2).reshape(R, N)


def _flash_step(q, k, v, bias, carry, mask, scale):
    m, l, acc = carry
    s = _dot_nt(q, k) * scale + bias
    if mask is not None:
        s = jnp.where(mask, s, NEG)
    m_new = jnp.maximum(m, jnp.max(s, axis=-1, keepdims=True))
    alpha = jnp.exp(m - m_new)
    p = jnp.exp(s - m_new)
    l = alpha * l + jnp.sum(p, axis=-1, keepdims=True)
    acc = alpha * acc + _dot(p.astype(BF16), v)
    return m_new, l, acc


def _flash_init(tq):
    return (jnp.full((tq, 1), NEG, F32), jnp.zeros((tq, 1), F32), jnp.zeros((tq, HEAD_DIM), F32))


def _causal_mask(tq, tk):
    rows = lax.broadcasted_iota(jnp.int32, (tq, tk), 0)
    cols = lax.broadcasted_iota(jnp.int32, (tq, tk), 1)
    return cols <= rows


def _fox_prompt_kernel(q_ref, k_ref, v_ref, cq_ref, ck_ref, o_ref, *, tq, scale):
    i = pl.program_id(2)
    q = q_ref[...]
    cq = cq_ref[...]

    def block(j, carry, mask):
        off = pl.multiple_of(j * tq, tq)
        k = k_ref[pl.ds(off, tq), :]
        v = v_ref[pl.ds(off, tq), :]
        return _flash_step(q, k, v, cq - ck_ref[j], carry, mask, scale)

    carry = lax.fori_loop(0, i, lambda j, c: block(j, c, None), _flash_init(tq))
    m, l, acc = block(i, carry, _causal_mask(tq, tq))
    o_ref[...] = (acc / l).astype(o_ref.dtype)


def _fox_prompt(q, k, v, c, B, S, tq=512):
    H = N_HEADS
    nq = S // tq
    c_col = c.reshape(B * H, S, 1)
    c_row = c.reshape(B * H, nq, 1, tq)
    kern = functools.partial(_fox_prompt_kernel, tq=tq, scale=HEAD_DIM ** -0.5)
    return pl.pallas_call(
        kern,
        grid=(B, H, nq),
        in_specs=[
            pl.BlockSpec((tq, HEAD_DIM), lambda b, h, i: (b * nq + i, h)),
            pl.BlockSpec((S, HEAD_DIM), lambda b, h, i: (b, h)),
            pl.BlockSpec((S, HEAD_DIM), lambda b, h, i: (b, h)),
            pl.BlockSpec((None, tq, 1), lambda b, h, i: (b * H + h, i, 0)),
            pl.BlockSpec((None, nq, 1, tq), lambda b, h, i: (b * H + h, 0, 0, 0)),
        ],
        out_specs=pl.BlockSpec((tq, HEAD_DIM), lambda b, h, i: (b * nq + i, h)),
        out_shape=jax.ShapeDtypeStruct((B * S, WIDTH), BF16),
        compiler_params=_params(3),
        name="fox_prompt",
    )(q, k, v, c_col, c_row)


def _fox_sample_kernel(q_ref, kc_ref, vc_ref, kn_ref, vn_ref, cq_ref, ckc_ref, ckn_ref, o_ref, *, tk, scale):
    T = q_ref.shape[0]
    q = q_ref[...]
    cq = cq_ref[...]

    def block(j, carry):
        off = pl.multiple_of(j * tk, tk)
        k = kc_ref[pl.ds(off, tk), :].astype(BF16)
        v = vc_ref[pl.ds(off, tk), :].astype(BF16)
        return _flash_step(q, k, v, cq - ckc_ref[j], carry, None, scale)

    carry = lax.fori_loop(0, kc_ref.shape[0] // tk, block, _flash_init(T))
    m, l, acc = _flash_step(q, kn_ref[...], vn_ref[...], cq - ckn_ref[...], carry, _causal_mask(T, T), scale)
    o_ref[...] = (acc / l).astype(o_ref.dtype)


def _fox_sample(q, k_new, v_new, cache_k, cache_v, c, B, T, tk=512):
    H = N_HEADS
    P = cache_k.shape[1]
    c_q = c[:, P:].reshape(B * H, T, 1)
    c_kc = c[:, :P].reshape(B * H, P // tk, 1, tk)
    c_kn = c[:, P:].reshape(B * H, 1, T)
    kern = functools.partial(_fox_sample_kernel, tk=tk, scale=HEAD_DIM ** -0.5)
    tok = pl.BlockSpec((T, HEAD_DIM), lambda b, h: (b, h))
    cache = pl.BlockSpec((None, P, HEAD_DIM), lambda b, h: (b, 0, h))
    return pl.pallas_call(
        kern,
        grid=(B, H),
        in_specs=[
            tok, cache, cache, tok, tok,
            pl.BlockSpec((None, T, 1), lambda b, h: (b * H + h, 0, 0)),
            pl.BlockSpec((None, P // tk, 1, tk), lambda b, h: (b * H + h, 0, 0, 0)),
            pl.BlockSpec((None, 1, T), lambda b, h: (b * H + h, 0, 0)),
        ],
        out_specs=tok,
        out_shape=jax.ShapeDtypeStruct((B * T, WIDTH), BF16),
        compiler_params=_params(2),
        name="fox_sample",
    )(q, cache_k, cache_v, k_new, v_new, c_q, c_kc, c_kn)


def _retention_kernel(q_ref, k_ref, v_ref, g_ref, gn_ref, s0_ref, dm_ref, qd_ref, kd_ref, cd_ref,
                      o_ref, s_ref, *, L):
    T = q_ref.shape[0]
    dmat = dm_ref[...]
    q_dec = qd_ref[...]
    k_dec = kd_ref[...]
    c_dec = cd_ref[...]
    gn = gn_ref[...]

    def body(c, s):
        off = pl.multiple_of(c * L, L)
        q = q_ref[pl.ds(off, L), :]
        k = k_ref[pl.ds(off, L), :]
        v = v_ref[pl.ds(off, L), :].astype(BF16)
        att = _dot_nt(q, k) * dmat
        o = _dot(att.astype(BF16), v) + _dot((q.astype(F32) * q_dec).astype(BF16), s.astype(BF16))
        kd = k.astype(F32) * k_dec
        s_new = s * c_dec + _dot(kd.T.astype(BF16), v)
        mu = jnp.mean(o, axis=-1, keepdims=True)
        oc = o - mu
        y = oc * lax.rsqrt(jnp.mean(oc * oc, axis=-1, keepdims=True) + EPS)
        o_ref[pl.ds(off, L), :] = (y * gn * _silu(g_ref[pl.ds(off, L), :])).astype(o_ref.dtype)
        return s_new

    s_ref[...] = lax.fori_loop(0, T // L, body, s0_ref[...])


def _retention_consts(L):
    H = N_HEADS
    log_g = jnp.log1p(-jnp.exp2(-5.0 - jnp.arange(H, dtype=F32)))
    i = jnp.arange(L, dtype=F32)
    diff = i[:, None] - i[None, :]
    dmat = jnp.where(diff[None] >= 0, jnp.exp(jnp.maximum(diff, 0.0)[None] * log_g[:, None, None]), 0.0)
    q_dec = jnp.exp((i + 1.0)[None, :] * log_g[:, None])
    k_dec = jnp.exp((L - 1.0 - i)[None, :] * log_g[:, None])
    c_dec = jnp.exp(L * log_g)
    rep = lambda a: jnp.broadcast_to(a[..., None], a.shape + (LANES,))
    return dmat, rep(q_dec), rep(k_dec), rep(c_dec[:, None])


def _retention(qk, vg, gn, s0, B, T):
    H = N_HEADS
    L = min(RET_CHUNK, T)
    dmat, q_dec, k_dec, c_dec = _retention_consts(L)
    seq = lambda off: pl.BlockSpec((T, HEAD_DIM), lambda b, h: (b, h + off))
    per_head = lambda r: pl.BlockSpec((None, r, LANES), lambda b, h: (h, 0, 0))
    state = pl.BlockSpec((None, None, HEAD_DIM, HEAD_DIM), lambda b, h: (b, h, 0, 0))
    o, s = pl.pallas_call(
        functools.partial(_retention_kernel, L=L),
        grid=(B, H),
        in_specs=[seq(0), seq(H), seq(0), seq(H),
                  pl.BlockSpec((1, HEAD_DIM), lambda b, h: (0, h)), state,
                  pl.BlockSpec((None, L, L), lambda b, h: (h, 0, 0)), per_head(L), per_head(L), per_head(1)],
        out_specs=[seq(0), state],
        out_shape=[jax.ShapeDtypeStruct((B * T, WIDTH), BF16), jax.ShapeDtypeStruct((B, H, HEAD_DIM, HEAD_DIM), F32)],
        compiler_params=_params(2),
        name="retention",
    )(qk, qk, vg, vg, gn.reshape(1, WIDTH), s0, dmat, q_dec, k_dec, c_dec)
    return o, s


def _conv_kernel(x_ref, prev_ref, init_ref, w_ref, b_ref, g_ref, beta_ref, o_ref, xp_ref, z_ref, *, tr):
    i = pl.program_id(1)

    @pl.when(i == 0)
    def _():
        xp_ref[0:HALO, :] = init_ref[...]

    @pl.when(i > 0)
    def _():
        xp_ref[0:HALO, :] = prev_ref[...]

    xp_ref[HALO:HALO + tr, :] = x_ref[...]
    C = x_ref.shape[1]
    rb = 32
    first = HALO - (CONV_WIDTH - 1)
    for c in range(C // LANES):
        cs = slice(c * LANES, (c + 1) * LANES)
        for r in range(tr // rb):
            acc = jnp.broadcast_to(b_ref[:, cs], (rb, LANES))
            for w in range(CONV_WIDTH):
                acc = acc + xp_ref[pl.ds(first + r * rb + w, rb), cs] * w_ref[w:w + 1, cs]
            z_ref[r * rb:(r + 1) * rb, cs] = acc
    z = z_ref[...]
    mu = jnp.mean(z, axis=-1, keepdims=True)
    zc = z - mu
    y = zc * lax.rsqrt(jnp.mean(zc * zc, axis=-1, keepdims=True) + EPS) * g_ref[...] + beta_ref[...]
    o_ref[...] = _silu(y).astype(o_ref.dtype)


def _conv_module(glu, init, w_dw, b_dw, ln_g, ln_b, B, T, tr):
    C = glu.shape[1]
    nt = T // tr
    per = tr // HALO
    w_pad = jnp.pad(w_dw, ((0, HALO - CONV_WIDTH), (0, 0)))
    vec = pl.BlockSpec((1, C), lambda b, i: (0, 0))
    return pl.pallas_call(
        functools.partial(_conv_kernel, tr=tr),
        grid=(B, nt),
        in_specs=[
            pl.BlockSpec((tr, C), lambda b, i: (b * nt + i, 0)),
            pl.BlockSpec((HALO, C), lambda b, i: (b * nt * per + jnp.maximum(i * per - 1, 0), 0)),
            pl.BlockSpec((None, HALO, C), lambda b, i: (b, 0, 0)),
            pl.BlockSpec((HALO, C), lambda b, i: (0, 0)),
            vec, vec, vec,
        ],
        out_specs=pl.BlockSpec((tr, C), lambda b, i: (b * nt + i, 0)),
        out_shape=jax.ShapeDtypeStruct((B * T, C), BF16),
        scratch_shapes=[pltpu.VMEM((HALO + tr, C), F32), pltpu.VMEM((tr, C), F32)],
        compiler_params=_params(2),
        name="conv_module",
    )(glu, glu, init, w_pad, b_dw.reshape(1, C), ln_g.reshape(1, C), ln_b.reshape(1, C))


def _rope_tables(pos):
    half = HEAD_DIM // 2
    inv = jnp.exp(-math.log(ROPE_BASE) * jnp.arange(half, dtype=F32) / half)
    ang = pos.astype(F32)[:, None] * inv[None, :]
    cos, sin = jnp.cos(ang), jnp.sin(ang)
    return jnp.concatenate([cos, cos], axis=-1), jnp.concatenate([-sin, sin], axis=-1)


def _prepare_weights(norm_mix, norm_ffn, w_in, b_forget, q_norm_gain, k_norm_gain, ret_norm_gain, w_out,
                     w_pw1, b_pw1, w_dw, b_dw, conv_ln_gain, conv_ln_bias, w_pw2, b_pw2,
                     w_ffn_gate, w_ffn_up, w_ffn_down):
    W = WIDTH
    wi = w_in[0]
    bf = lambda a: a.astype(BF16)
    f0 = 3 * W
    f1 = f0 + N_HEADS
    return dict(
        norm_mix=norm_mix, norm_ffn=norm_ffn,
        w_q=bf(wi[:, :W]), w_k=bf(wi[:, W:2 * W]), w_v=bf(wi[:, 2 * W:f0]),
        w_f=bf(jnp.pad(wi[:, f0:f1], ((0, 0), (0, LANES - N_HEADS)))),
        b_f=jnp.pad(b_forget[0], (0, LANES - N_HEADS)).reshape(1, LANES),
        w_qkb=bf(wi[:, f1:f1 + 2 * W]), w_vgb=bf(wi[:, f1 + 2 * W:]),
        rope_scale=jnp.concatenate([jnp.ones((1, W), F32), jnp.full((1, W), HEAD_DIM ** -0.5, F32)], axis=1),
        gq=q_norm_gain[0].reshape(1, HEAD_DIM), gk=k_norm_gain[0].reshape(1, HEAD_DIM), gn=ret_norm_gain[0],
        w_oa=bf(w_out[0, :W]), w_ob=bf(w_out[0, W:]),
        w_1a=bf(w_pw1[0, :, :2 * W]), w_1g=bf(w_pw1[0, :, 2 * W:]),
        b_1a=b_pw1[0, :2 * W].reshape(1, -1), b_1g=b_pw1[0, 2 * W:].reshape(1, -1),
        w_dw=w_dw[0], b_dw=b_dw[0], ln_g=conv_ln_gain[0], ln_b=conv_ln_bias[0],
        w_2=bf(w_pw2[0]), b_2=b_pw2[0].reshape(1, -1),
        w_g=bf(w_ffn_gate), w_u=bf(w_ffn_up), w_d=bf(w_ffn_down),
    )


def _ffn(x, P, layer, tag):
    h = _rmsnorm(x, P["norm_ffn"][layer], f"rms_ffn{layer}_{tag}")
    act, = _matmul(f"ffn_up{layer}_{tag}", [h], [P["w_g"][layer], P["w_u"][layer]], [[(0, 0)], [(0, 1)]],
                   _ep_swiglu, [BF16])
    out, = _matmul(f"ffn_down{layer}_{tag}", [act], [P["w_d"][layer]], [[(0, 0)]], _ep_residual, [F32], tile=[x])
    return out


def _trunk(x3, pos, past, conv_past, P, tag):
    B, T, D = x3.shape
    M = B * T
    H = N_HEADS
    x = x3.reshape(M, D)
    one = [[(0, 0)]]

    h = _rmsnorm(x, P["norm_mix"][0], f"rms_mix0_{tag}")
    q, = _matmul(f"proj_q_{tag}", [h], [P["w_q"]], one, _ep_headnorm, [BF16], full=[P["gq"]])
    k32, k16 = _matmul(f"proj_k_{tag}", [h], [P["w_k"]], one, _ep_headnorm, [F32, BF16], full=[P["gk"]])
    v32, v16 = _matmul(f"proj_v_{tag}", [h], [P["w_v"]], one, _ep_identity, [F32, BF16])
    logf_pad, = _matmul(f"proj_f_{tag}", [h], [P["w_f"]], one, _ep_logsigmoid, [F32], col=[P["b_f"]], tn=LANES)
    cos, sin = _rope_tables(pos)
    reps = max(1, 512 // T)
    cos, sin = jnp.tile(cos, (reps, 1)), jnp.tile(sin, (reps, 1))
    qk_b, = _matmul(f"proj_qkb_{tag}", [h], [P["w_qkb"]], one, _ep_rope, [BF16], col=[P["rope_scale"]],
                    row=[cos, sin])
    vg_b, = _matmul(f"proj_vgb_{tag}", [h], [P["w_vgb"]], one, _ep_identity, [F32])

    logf = logf_pad[:, :H].reshape(B, T, H)
    lf_t = logf.transpose(0, 2, 1).reshape(B * H, T)
    if past is None:
        c = _cumsum_rows(lf_t, f"cumsum_{tag}")
        oa = _fox_prompt(q, k16, v16, c, B, T)
        s0 = jnp.zeros((B, H, HEAD_DIM, HEAD_DIM), F32)
    else:
        kc, vc, lfc, s0 = past
        Pl = kc.shape[1]
        lf_all = jnp.concatenate([lfc.transpose(0, 2, 1).reshape(B * H, Pl), lf_t], axis=1)
        pad = (-lf_all.shape[1]) % LANES
        c = _cumsum_rows(jnp.pad(lf_all, ((0, 0), (0, pad))), f"cumsum_{tag}")[:, :Pl + T]
        oa = _fox_sample(q, k16, v16, kc.reshape(B, Pl, WIDTH), vc.reshape(B, Pl, WIDTH), c, B, T)
    ob, s_new = _retention(qk_b, vg_b, P["gn"], s0, B, T)
    x, = _matmul(f"out_proj_{tag}", [oa, ob], [P["w_oa"], P["w_ob"]], [[(0, 0), (1, 1)]], _ep_residual, [F32],
                 tile=[x])
    x = _ffn(x, P, 0, tag)

    h = _rmsnorm(x, P["norm_mix"][1], f"rms_mix1_{tag}")
    glu, = _matmul(f"conv_pw1_{tag}", [h], [P["w_1a"], P["w_1g"]], [[(0, 0)], [(0, 1)]], _ep_glu, [F32],
                   col=[P["b_1a"], P["b_1g"]])
    keep = CONV_WIDTH - 1
    if conv_past is None:
        init = jnp.zeros((B, HALO, D), F32)
    else:
        init = jnp.pad(conv_past, ((0, 0), (HALO - keep, 0), (0, 0)))
    z = _conv_module(glu, init, P["w_dw"], P["b_dw"], P["ln_g"], P["ln_b"], B, T, tr=min(128, T))
    glu3 = glu.reshape(B, T, D)
    assert T >= keep
    conv_new = glu3[:, -keep:]
    x, = _matmul(f"conv_pw2_{tag}", [z], [P["w_2"]], one, _ep_bias_residual, [F32], tile=[x], col=[P["b_2"]])
    x = _ffn(x, P, 1, tag)

    return (x.reshape(B, T, D), k32.reshape(1, B, T, H, HEAD_DIM), v32.reshape(1, B, T, H, HEAD_DIM),
            logf[None], s_new[None], conv_new[None])


def kernel(x_prompt, x_sample, cache_k, cache_v, cache_logf, state_ret, state_conv,
           norm_mix, norm_ffn, w_in, b_forget, q_norm_gain, k_norm_gain, ret_norm_gain, w_out,
           w_pw1, b_pw1, w_dw, b_dw, conv_ln_gain, conv_ln_bias, w_pw2, b_pw2,
           w_ffn_gate, w_ffn_up, w_ffn_down):
    P = _prepare_weights(norm_mix, norm_ffn, w_in, b_forget, q_norm_gain, k_norm_gain, ret_norm_gain, w_out,
                         w_pw1, b_pw1, w_dw, b_dw, conv_ln_gain, conv_ln_bias, w_pw2, b_pw2,
                         w_ffn_gate, w_ffn_up, w_ffn_down)
    pos_p = jnp.arange(x_prompt.shape[1])
    y_p, k_p, v_p, lf_p, r_p, c_p = _trunk(x_prompt, pos_p, None, None, P, "p")
    pos_s = cache_k.shape[2] + jnp.arange(x_sample.shape[1])
    past = (cache_k[0], cache_v[0], cache_logf[0], state_ret[0])
    y_s, k_s, v_s, lf_s, r_s, c_s = _trunk(x_sample, pos_s, past, state_conv[0], P, "s")
    return (y_p, y_s, k_p, v_p, lf_p, r_p, c_p, k_s, v_s, lf_s, r_s, c_s)
```

```python
import functools
import math
from typing import NamedTuple, Optional

import jax
import jax.numpy as jnp
from jax import lax
from jax.experimental import pallas as pl
from jax.experimental.pallas import tpu as pltpu

F32 = jnp.float32
BF16 = jnp.bfloat16

LANES = 128
SUBLANES = 8
HALO = 32
VMEM_LIMIT = 56 * 1024 * 1024

HEAD_DIM = 128
N_HEADS = 8
WIDTH = N_HEADS * HEAD_DIM
CONV_WIDTH = 31
ROPE_BASE = 10000.0
EPS = 1e-6
NEG = -1e30
LOG2E = math.log2(math.e)
RET_CHUNK = 256
RET_UNROLL = 4


def _params(n_grid):
    return pltpu.CompilerParams(dimension_semantics=("arbitrary",) * n_grid, vmem_limit_bytes=VMEM_LIMIT)


def _silu(x):
    return x * jax.nn.sigmoid(x)


def _dot(a, b):
    return jnp.dot(a, b, preferred_element_type=F32)


def _dot_nt(a, b):
    return lax.dot_general(a, b, (((1,), (1,)), ((), ())), preferred_element_type=F32)


def _rms_kernel(x_ref, g_ref, o_ref):
    x = x_ref[...]
    ms = jnp.mean(x * x, axis=-1, keepdims=True)
    o_ref[...] = (x * lax.rsqrt(ms + EPS) * g_ref[...]).astype(o_ref.dtype)


def _rmsnorm(x, g, name, tm=512):
    M, D = x.shape
    return pl.pallas_call(
        _rms_kernel,
        grid=(M // tm,),
        in_specs=[pl.BlockSpec((tm, D), lambda i: (i, 0)), pl.BlockSpec((1, D), lambda i: (0, 0))],
        out_specs=pl.BlockSpec((tm, D), lambda i: (i, 0)),
        out_shape=jax.ShapeDtypeStruct((M, D), BF16),
        compiler_params=_params(1),
        name=name,
    )(x, g.reshape(1, D))


class _Rhs(NamedTuple):
    arr: jax.Array
    lead: Optional[int]
    k: int
    k_blk: int = 0
    col: int = 0


def _mm_kernel(*refs, n_lhs, n_rhs, n_extra, n_out, products, epilogue):
    lhs = refs[:n_lhs]
    rhs = refs[n_lhs:n_lhs + n_rhs]
    extras = refs[n_lhs + n_rhs:n_lhs + n_rhs + n_extra]
    outs = refs[n_lhs + n_rhs + n_extra:n_lhs + n_rhs + n_extra + n_out]
    wbuf = refs[n_lhs + n_rhs + n_extra + n_out:]

    @pl.when(pl.program_id(1) == 0)
    def _():
        for w, b in zip(rhs, wbuf):
            b[...] = w[...].astype(BF16)

    accs = []
    for prod in products:
        acc = None
        for a, b in prod:
            d = _dot(lhs[a][...], wbuf[b][...])
            acc = d if acc is None else acc + d
        accs.append(acc)
    epilogue(accs, extras, outs)


def _matmul(name, lhs, rhs, n_cols, products, epilogue, out_dtypes, *, tile=(), col=(), row=(), full=(),
            tm=512, tn=512):
    M = lhs[0].shape[0]
    assert M % tm == 0 and n_cols % tn == 0
    in_specs = [pl.BlockSpec((tm, a.shape[1]), lambda j, i: (i, 0)) for a in lhs]
    for w in rhs:
        assert w.col % tn == 0
        c0 = w.col // tn
        if w.lead is None:
            in_specs.append(pl.BlockSpec((w.k, tn), lambda j, i, kb=w.k_blk, c0=c0: (kb, j + c0)))
        else:
            in_specs.append(pl.BlockSpec((None, w.k, tn),
                                         lambda j, i, ld=w.lead, kb=w.k_blk, c0=c0: (ld, kb, j + c0)))
    in_specs += [pl.BlockSpec((tm, tn), lambda j, i: (i, j)) for _ in tile]
    in_specs += [pl.BlockSpec((1, tn), lambda j, i: (0, j)) for _ in col]
    for r in row:
        assert r.shape[0] % tm == 0
        in_specs.append(pl.BlockSpec((tm, r.shape[1]), lambda j, i, nblk=r.shape[0] // tm: (i % nblk, 0)))
    in_specs += [pl.BlockSpec(f.shape, lambda j, i, nd=f.ndim: (0,) * nd) for f in full]
    extras = tuple(tile) + tuple(col) + tuple(row) + tuple(full)
    kern = functools.partial(_mm_kernel, n_lhs=len(lhs), n_rhs=len(rhs), n_extra=len(extras),
                             n_out=len(out_dtypes), products=products, epilogue=epilogue)
    return pl.pallas_call(
        kern,
        grid=(n_cols // tn, M // tm),
        in_specs=in_specs,
        out_specs=[pl.BlockSpec((tm, tn), lambda j, i: (i, j)) for _ in out_dtypes],
        out_shape=[jax.ShapeDtypeStruct((M, n_cols), dt) for dt in out_dtypes],
        scratch_shapes=[pltpu.VMEM((w.k, tn), BF16) for w in rhs],
        compiler_params=_params(2),
        name=name,
    )(*lhs, *[w.arr for w in rhs], *extras)


def _ep_headnorm(accs, extras, outs, *, scales):
    gain = extras[0][...]
    y = accs[0]
    for h in range(y.shape[1] // HEAD_DIM):
        sl = slice(h * HEAD_DIM, (h + 1) * HEAD_DIM)
        yh = y[:, sl]
        ms = jnp.mean(yh * yh, axis=-1, keepdims=True)
        r = yh * lax.rsqrt(ms + EPS) * gain
        for o, sc in zip(outs, scales):
            o[:, sl] = (r if sc == 1.0 else r * sc).astype(o.dtype)


def _ep_identity(accs, extras, outs):
    for o in outs:
        o[...] = accs[0].astype(o.dtype)


def _ep_logsigmoid(accs, extras, outs):
    z = accs[0] + extras[0][...]
    outs[0][...] = -(jnp.maximum(-z, 0.0) + jnp.log1p(jnp.exp(-jnp.abs(z))))


def _ep_rope(accs, extras, outs):
    scale, cos, sin = extras[0][...], extras[1][...], extras[2][...]
    y = accs[0]
    for h in range(y.shape[1] // HEAD_DIM):
        sl = slice(h * HEAD_DIM, (h + 1) * HEAD_DIM)
        yh = y[:, sl]
        r = yh * cos + pltpu.roll(yh, HEAD_DIM // 2, 1) * sin
        outs[0][:, sl] = (r * scale[:, sl]).astype(outs[0].dtype)


def _ep_residual(accs, extras, outs):
    outs[0][...] = extras[0][...] + accs[0]


def _ep_bias_residual(accs, extras, outs):
    outs[0][...] = extras[0][...] + (accs[0] + extras[1][...])


def _ep_swiglu(accs, extras, outs):
    outs[0][...] = (_silu(accs[0]) * accs[1]).astype(outs[0].dtype)


def _ep_glu(accs, extras, outs):
    a = accs[0] + extras[0][...]
    g = accs[1] + extras[1][...]
    outs[0][...] = a * jax.nn.sigmoid(g)


def _split3(x):
    hi = x.astype(BF16)
    r1 = x - hi.astype(F32)
    mid = r1.astype(BF16)
    lo = (r1 - mid.astype(F32)).astype(BF16)
    return hi, mid, lo


def _cumsum_kernel(x_ref, c_ref, hi_ref, mid_ref, lo_ref, *, scale):
    nblk, R, _ = x_ref.shape
    row = lax.broadcasted_iota(jnp.int32, (LANES, LANES), 0)
    col = lax.broadcasted_iota(jnp.int32, (LANES, LANES), 1)
    tri = jnp.where(row <= col, 1.0, 0.0).astype(BF16)
    ones = jnp.ones((LANES, LANES), BF16)

    def body(b, carry):
        hi, mid, lo = _split3(x_ref[b])
        within = _dot(hi, tri) + _dot(mid, tri) + _dot(lo, tri)
        total = _dot(hi, ones) + _dot(mid, ones) + _dot(lo, ones)
        c = (carry + within) * scale
        c_ref[b] = c
        hi_ref[b], mid_ref[b], lo_ref[b] = _split3(c)
        return carry + total

    lax.fori_loop(0, nblk, body, jnp.zeros((R, LANES), F32))


def _cumsum_rows(x, name, scale):
    R, N = x.shape
    nblk = N // LANES
    x3 = x.reshape(R, nblk, LANES).transpose(1, 0, 2)
    outs = pl.pallas_call(
        functools.partial(_cumsum_kernel, scale=scale),
        out_shape=[jax.ShapeDtypeStruct((nblk, R, LANES), dt) for dt in (F32, BF16, BF16, BF16)],
        compiler_params=pltpu.CompilerParams(vmem_limit_bytes=VMEM_LIMIT),
        name=name,
    )(x3)
    c, hi, mid, lo = [o.transpose(1, 0, 2).reshape(R, N) for o in outs]
    return c, (hi, mid, lo)


def _flash_step(s, v, carry, mask):
    m, l, acc = carry
    if mask is not None:
        s = jnp.where(mask, s, NEG)
    m_new = jnp.maximum(m, jnp.max(s, axis=-1, keepdims=True))
    alpha = jnp.exp2(m - m_new)
    p = jnp.exp2(s - m_new)
    l = alpha * l + jnp.sum(p, axis=-1, keepdims=True)
    acc = alpha * acc + _dot(p.astype(BF16), v)
    return m_new, l, acc


def _flash_init(tq):
    return (jnp.full((tq, 1), NEG, F32), jnp.zeros((tq, 1), F32), jnp.zeros((tq, HEAD_DIM), F32))


def _causal_mask(tq, tk):
    rows = lax.broadcasted_iota(jnp.int32, (tq, tk), 0)
    cols = lax.broadcasted_iota(jnp.int32, (tq, tk), 1)
    return cols <= rows


def _fox_prompt_kernel(q_ref, qb_ref, k_ref, kb_ref, v_ref, o_ref, *, tq):
    i = pl.program_id(2)
    q = jnp.concatenate([q_ref[...], qb_ref[...]], axis=1)

    def block(j, carry, mask):
        off = pl.multiple_of(j * tq, tq)
        k = jnp.concatenate([k_ref[pl.ds(off, tq), :], kb_ref[pl.ds(off, tq), :]], axis=1)
        return _flash_step(_dot_nt(q, k), v_ref[pl.ds(off, tq), :], carry, mask)

    carry = lax.fori_loop(0, i, lambda j, c: block(j, c, None), _flash_init(tq))
    m, l, acc = block(i, carry, _causal_mask(tq, tq))
    o_ref[...] = (acc / l).astype(o_ref.dtype)


def _bias_columns(pieces):
    parts = jnp.stack(pieces, axis=-1)
    ones = jnp.ones_like(parts)
    pad = ((0, 0), (0, 0), (0, LANES - 6))
    return (jnp.pad(jnp.concatenate([parts, ones], axis=-1), pad),
            jnp.pad(jnp.concatenate([ones, -parts], axis=-1), pad))


def _fox_prompt(q, k, v, c2_pieces, B, S, tq=512):
    H = N_HEADS
    nq = S // tq
    qb, kb = _bias_columns(c2_pieces)
    return pl.pallas_call(
        functools.partial(_fox_prompt_kernel, tq=tq),
        grid=(B, H, nq),
        in_specs=[
            pl.BlockSpec((tq, HEAD_DIM), lambda b, h, i: (b * nq + i, h)),
            pl.BlockSpec((None, tq, LANES), lambda b, h, i: (b * H + h, i, 0)),
            pl.BlockSpec((S, HEAD_DIM), lambda b, h, i: (b, h)),
            pl.BlockSpec((None, S, LANES), lambda b, h, i: (b * H + h, 0, 0)),
            pl.BlockSpec((S, HEAD_DIM), lambda b, h, i: (b, h)),
        ],
        out_specs=pl.BlockSpec((tq, HEAD_DIM), lambda b, h, i: (b * nq + i, h)),
        out_shape=jax.ShapeDtypeStruct((B * S, WIDTH), BF16),
        compiler_params=_params(3),
        name="fox_prompt",
    )(q, qb, k, kb, v)


def _fox_sample_kernel(q_ref, kc_ref, vc_ref, kn_ref, vn_ref, cq_ref, ckc_ref, ckn_ref, o_ref):
    T = q_ref.shape[0]
    q = q_ref[...]
    cq = cq_ref[...]
    s_old = _dot_nt(q, kc_ref[...].astype(BF16)) + (cq - ckc_ref[...])
    s_new = jnp.where(_causal_mask(T, T), _dot_nt(q, kn_ref[...]) + (cq - ckn_ref[...]), NEG)
    m = jnp.maximum(jnp.max(s_old, axis=-1, keepdims=True), jnp.max(s_new, axis=-1, keepdims=True))
    p_old = jnp.exp2(s_old - m)
    p_new = jnp.exp2(s_new - m)
    l = jnp.sum(p_old, axis=-1, keepdims=True) + jnp.sum(p_new, axis=-1, keepdims=True)
    acc = _dot(p_old.astype(BF16), vc_ref[...].astype(BF16)) + _dot(p_new.astype(BF16), vn_ref[...])
    o_ref[...] = (acc / l).astype(o_ref.dtype)


def _fox_sample(q, k_new, v_new, cache_k, cache_v, c2, B, T):
    H = N_HEADS
    P = cache_k.shape[1]
    c_q = c2[:, P:].reshape(B * H, T, 1)
    c_kc = c2[:, :P].reshape(B * H, 1, P)
    c_kn = c2[:, P:].reshape(B * H, 1, T)
    tok = pl.BlockSpec((T, HEAD_DIM), lambda b, h: (b, h))
    cache = pl.BlockSpec((None, P, HEAD_DIM), lambda b, h: (b, 0, h))
    return pl.pallas_call(
        _fox_sample_kernel,
        grid=(B, H),
        in_specs=[
            tok, cache, cache, tok, tok,
            pl.BlockSpec((None, T, 1), lambda b, h: (b * H + h, 0, 0)),
            pl.BlockSpec((None, 1, P), lambda b, h: (b * H + h, 0, 0)),
            pl.BlockSpec((None, 1, T), lambda b, h: (b * H + h, 0, 0)),
        ],
        out_specs=tok,
        out_shape=jax.ShapeDtypeStruct((B * T, WIDTH), BF16),
        compiler_params=_params(2),
        name="fox_sample",
    )(q, cache_k, cache_v, k_new, v_new, c_q, c_kc, c_kn)


def _retention_kernel(q_ref, k_ref, v_ref, g_ref, gn_ref, s0_ref, dm_ref, qd_ref, kd_ref, cd_ref,
                      o_ref, s_ref, *, L, hb, unroll):
    T = q_ref.shape[0]

    def chunk(off, h, s):
        rows = pl.ds(off, L)
        hs = slice(h * HEAD_DIM, (h + 1) * HEAD_DIM)
        q = q_ref[rows, hs]
        k = k_ref[rows, hs]
        v = v_ref[rows, hs].astype(BF16)
        att = _dot_nt(q, k) * dm_ref[h]
        o = _dot(att.astype(BF16), v) + _dot((q.astype(F32) * qd_ref[h]).astype(BF16), s.astype(BF16))
        kd = k.astype(F32) * kd_ref[h]
        s_new = s * cd_ref[h] + _dot(kd.T.astype(BF16), v)
        mu = jnp.mean(o, axis=-1, keepdims=True)
        oc = o - mu
        y = oc * lax.rsqrt(jnp.mean(oc * oc, axis=-1, keepdims=True) + EPS)
        o_ref[rows, hs] = (y * gn_ref[:, hs] * _silu(g_ref[rows, hs])).astype(o_ref.dtype)
        return s_new

    def body(it, states):
        states = list(states)
        for u in range(unroll):
            off = pl.multiple_of((it * unroll + u) * L, L)
            for h in range(hb):
                states[h] = chunk(off, h, states[h])
        return tuple(states)

    states = lax.fori_loop(0, T // (L * unroll), body, tuple(s0_ref[h] for h in range(hb)))
    for h in range(hb):
        s_ref[h] = states[h]


def _retention_consts(L):
    H = N_HEADS
    log_g = jnp.log1p(-jnp.exp2(-5.0 - jnp.arange(H, dtype=F32)))
    i = jnp.arange(L, dtype=F32)
    diff = i[:, None] - i[None, :]
    dmat = jnp.where(diff[None] >= 0, jnp.exp(jnp.maximum(diff, 0.0)[None] * log_g[:, None, None]), 0.0)
    q_dec = jnp.exp((i + 1.0)[None, :] * log_g[:, None])
    k_dec = jnp.exp((L - 1.0 - i)[None, :] * log_g[:, None])
    c_dec = jnp.exp(L * log_g)
    rep = lambda a: jnp.broadcast_to(a[..., None], a.shape + (LANES,))
    return dmat, rep(q_dec), rep(k_dec), rep(c_dec[:, None])


def _retention(qk, vg, gn, s0, B, T):
    H = N_HEADS
    L = min(RET_CHUNK, T)
    n_chunks = T // L
    unroll = math.gcd(RET_UNROLL, n_chunks)
    hb = 1 if n_chunks > 1 else H
    nh = H // hb
    dmat, q_dec, k_dec, c_dec = _retention_consts(L)
    seq = lambda off: pl.BlockSpec((T, hb * HEAD_DIM), lambda b, h: (b, h + off))
    per_head = lambda r: pl.BlockSpec((hb, r, LANES), lambda b, h: (h, 0, 0))
    state = pl.BlockSpec((None, hb, HEAD_DIM, HEAD_DIM), lambda b, h: (b, h, 0, 0))
    return pl.pallas_call(
        functools.partial(_retention_kernel, L=L, hb=hb, unroll=unroll),
        grid=(B, nh),
        in_specs=[seq(0), seq(nh), seq(0), seq(nh),
                  pl.BlockSpec((1, hb * HEAD_DIM), lambda b, h: (0, h)), state,
                  pl.BlockSpec((hb, L, L), lambda b, h: (h, 0, 0)), per_head(L), per_head(L), per_head(1)],
        out_specs=[seq(0), state],
        out_shape=[jax.ShapeDtypeStruct((B * T, WIDTH), BF16), jax.ShapeDtypeStruct((B, H, HEAD_DIM, HEAD_DIM), F32)],
        compiler_params=_params(2),
        name="retention",
    )(qk, qk, vg, vg, gn.reshape(1, WIDTH), s0, dmat, q_dec, k_dec, c_dec)


def _conv_kernel(x_ref, prev_ref, init_ref, w_ref, b_ref, g_ref, beta_ref, o_ref, xp_ref, xs_ref, z_ref, *, tr):
    i = pl.program_id(1)

    @pl.when(i == 0)
    def _():
        xp_ref[0:HALO, :] = init_ref[...]

    @pl.when(i > 0)
    def _():
        xp_ref[0:HALO, :] = prev_ref[...]

    xp_ref[HALO:HALO + tr, :] = x_ref[...]
    C = x_ref.shape[1]
    first = HALO - (CONV_WIDTH - 1)
    for r in range(SUBLANES):
        span = tr + SUBLANES * ((CONV_WIDTH - 1 - r) // SUBLANES)
        xs_ref[r, 0:span, :] = xp_ref[pl.ds(first + r, span), :]

    rb = 32

    def rows(rblk, carry):
        r0 = pl.multiple_of(rblk * rb, rb)
        for c in range(C // LANES):
            cs = slice(c * LANES, (c + 1) * LANES)
            acc = jnp.broadcast_to(b_ref[:, cs], (rb, LANES))
            for w in range(CONV_WIDTH):
                a, r = divmod(w, SUBLANES)
                wv = jnp.concatenate([w_ref[w, :, cs]] * (rb // SUBLANES), axis=0)
                acc = acc + xs_ref[r, pl.ds(r0 + SUBLANES * a, rb), cs] * wv
            z_ref[pl.ds(r0, rb), cs] = acc
        return carry

    lax.fori_loop(0, tr // rb, rows, 0)
    z = z_ref[...]
    mu = jnp.mean(z, axis=-1, keepdims=True)
    zc = z - mu
    y = zc * lax.rsqrt(jnp.mean(zc * zc, axis=-1, keepdims=True) + EPS) * g_ref[...] + beta_ref[...]
    o_ref[...] = _silu(y).astype(o_ref.dtype)


def _conv_module(glu, init, w_dw, b_dw, ln_g, ln_b, B, T, tr):
    C = glu.shape[1]
    nt = T // tr
    per = tr // HALO
    w_rep = jnp.broadcast_to(jnp.pad(w_dw, ((0, HALO - CONV_WIDTH), (0, 0)))[:, None, :], (HALO, SUBLANES, C))
    vec = pl.BlockSpec((1, C), lambda b, i: (0, 0))
    return pl.pallas_call(
        functools.partial(_conv_kernel, tr=tr),
        grid=(B, nt),
        in_specs=[
            pl.BlockSpec((tr, C), lambda b, i: (b * nt + i, 0)),
            pl.BlockSpec((HALO, C), lambda b, i: (b * nt * per + jnp.maximum(i * per - 1, 0), 0)),
            pl.BlockSpec((None, HALO, C), lambda b, i: (b, 0, 0)),
            pl.BlockSpec((HALO, SUBLANES, C), lambda b, i: (0, 0, 0)),
            vec, vec, vec,
        ],
        out_specs=pl.BlockSpec((tr, C), lambda b, i: (b * nt + i, 0)),
        out_shape=jax.ShapeDtypeStruct((B * T, C), BF16),
        scratch_shapes=[pltpu.VMEM((HALO + tr, C), F32),
                        pltpu.VMEM((SUBLANES, tr + HALO - SUBLANES, C), F32),
                        pltpu.VMEM((tr, C), F32)],
        compiler_params=_params(2),
        name="conv_module",
    )(glu, glu, init, w_rep, b_dw.reshape(1, C), ln_g.reshape(1, C), ln_b.reshape(1, C))


def _rope_tables(pos):
    half = HEAD_DIM // 2
    inv = jnp.exp(-math.log(ROPE_BASE) * jnp.arange(half, dtype=F32) / half)
    ang = pos.astype(F32)[:, None] * inv[None, :]
    cos, sin = jnp.cos(ang), jnp.sin(ang)
    return jnp.concatenate([cos, cos], axis=-1), jnp.concatenate([-sin, sin], axis=-1)


def _prepare_weights(norm_mix, norm_ffn, w_in, b_forget, q_norm_gain, k_norm_gain, ret_norm_gain, w_out,
                     w_pw1, b_pw1, w_dw, b_dw, conv_ln_gain, conv_ln_bias, w_pw2, b_pw2,
                     w_ffn_gate, w_ffn_up, w_ffn_down):
    W = WIDTH
    D = w_in.shape[1]
    f0 = 3 * W
    f1 = f0 + N_HEADS
    w_b = w_in[0, :, f1:]
    return dict(
        norm_mix=norm_mix, norm_ffn=norm_ffn,
        w_q=_Rhs(w_in, 0, D, col=0), w_k=_Rhs(w_in, 0, D, col=W), w_v=_Rhs(w_in, 0, D, col=2 * W),
        w_f=_Rhs(w_in, 0, D, col=f0),
        b_f=jnp.pad(b_forget[0], (0, LANES - N_HEADS)).reshape(1, LANES),
        w_qkb=_Rhs(w_b, None, D, col=0), w_vgb=_Rhs(w_b, None, D, col=2 * W),
        rope_scale=jnp.concatenate([jnp.ones((1, W), F32), jnp.full((1, W), HEAD_DIM ** -0.5, F32)], axis=1),
        gq=q_norm_gain[0].reshape(1, HEAD_DIM), gk=k_norm_gain[0].reshape(1, HEAD_DIM), gn=ret_norm_gain[0],
        w_oa=_Rhs(w_out, 0, W, k_blk=0), w_ob=_Rhs(w_out, 0, W, k_blk=1),
        w_1a=_Rhs(w_pw1, 0, D, col=0), w_1g=_Rhs(w_pw1, 0, D, col=w_pw1.shape[2] // 2),
        b_1a=b_pw1[0, :2 * W].reshape(1, -1), b_1g=b_pw1[0, 2 * W:].reshape(1, -1),
        w_dw=w_dw[0], b_dw=b_dw[0], ln_g=conv_ln_gain[0], ln_b=conv_ln_bias[0],
        w_2=_Rhs(w_pw2, 0, w_pw2.shape[1]), b_2=b_pw2[0].reshape(1, -1),
        w_g=[_Rhs(w_ffn_gate, l, D) for l in range(2)], w_u=[_Rhs(w_ffn_up, l, D) for l in range(2)],
        w_d=[_Rhs(w_ffn_down, l, w_ffn_down.shape[1]) for l in range(2)],
        d_ff=w_ffn_gate.shape[2],
    )


def _ffn(x, P, layer, tag):
    D = x.shape[1]
    h = _rmsnorm(x, P["norm_ffn"][layer], f"rms_ffn{layer}_{tag}")
    act, = _matmul(f"ffn_up{layer}_{tag}", [h], [P["w_g"][layer], P["w_u"][layer]], P["d_ff"],
                   [[(0, 0)], [(0, 1)]], _ep_swiglu, [BF16])
    out, = _matmul(f"ffn_down{layer}_{tag}", [act], [P["w_d"][layer]], D, [[(0, 0)]], _ep_residual, [F32],
                   tile=[x])
    return out


def _trunk(x3, pos, past, conv_past, P, tag):
    B, T, D = x3.shape
    M = B * T
    H = N_HEADS
    W = WIDTH
    x = x3.reshape(M, D)
    one = [[(0, 0)]]
    wide = dict(tn=1024)

    h = _rmsnorm(x, P["norm_mix"][0], f"rms_mix0_{tag}")
    q, = _matmul(f"proj_q_{tag}", [h], [P["w_q"]], W, one,
                 functools.partial(_ep_headnorm, scales=(HEAD_DIM ** -0.5 * LOG2E,)), [BF16], full=[P["gq"]], **wide)
    k32, k16 = _matmul(f"proj_k_{tag}", [h], [P["w_k"]], W, one,
                       functools.partial(_ep_headnorm, scales=(1.0, 1.0)), [F32, BF16], full=[P["gk"]], **wide)
    v32, v16 = _matmul(f"proj_v_{tag}", [h], [P["w_v"]], W, one, _ep_identity, [F32, BF16], **wide)
    logf_pad, = _matmul(f"proj_f_{tag}", [h], [P["w_f"]], LANES, one, _ep_logsigmoid, [F32], col=[P["b_f"]],
                        tn=LANES)
    cos, sin = _rope_tables(pos)
    reps = max(1, 512 // T)
    cos, sin = jnp.tile(cos, (reps, 1)), jnp.tile(sin, (reps, 1))
    qk_b, = _matmul(f"proj_qkb_{tag}", [h], [P["w_qkb"]], 2 * W, one, _ep_rope, [BF16], col=[P["rope_scale"]],
                    row=[cos, sin], **wide)
    vg_b, = _matmul(f"proj_vgb_{tag}", [h], [P["w_vgb"]], 2 * W, one, _ep_identity, [F32], **wide)

    logf = logf_pad[:, :H].reshape(B, T, H)
    lf_t = logf.transpose(0, 2, 1).reshape(B * H, T)
    if past is None:
        _, c2_pieces = _cumsum_rows(lf_t, f"cumsum_{tag}", LOG2E)
        oa = _fox_prompt(q, k16, v16, c2_pieces, B, T)
        s0 = jnp.zeros((B, H, HEAD_DIM, HEAD_DIM), F32)
    else:
        kc, vc, lfc, s0 = past
        Pl = kc.shape[1]
        lf_all = jnp.concatenate([lfc.transpose(0, 2, 1).reshape(B * H, Pl), lf_t], axis=1)
        pad = (-lf_all.shape[1]) % LANES
        c2, _ = _cumsum_rows(jnp.pad(lf_all, ((0, 0), (0, pad))), f"cumsum_{tag}", LOG2E)
        oa = _fox_sample(q, k16, v16, kc, vc, c2[:, :Pl + T], B, T)
    ob, s_new = _retention(qk_b, vg_b, P["gn"], s0, B, T)
    x, = _matmul(f"out_proj_{tag}", [oa, ob], [P["w_oa"], P["w_ob"]], D, [[(0, 0), (1, 1)]], _ep_residual, [F32],
                 tile=[x], **wide)
    x = _ffn(x, P, 0, tag)

    h = _rmsnorm(x, P["norm_mix"][1], f"rms_mix1_{tag}")
    glu, = _matmul(f"conv_pw1_{tag}", [h], [P["w_1a"], P["w_1g"]], D, [[(0, 0)], [(0, 1)]], _ep_glu, [F32],
                   col=[P["b_1a"], P["b_1g"]])
    keep = CONV_WIDTH - 1
    assert T >= keep
    if conv_past is None:
        init = jnp.zeros((B, HALO, D), F32)
    else:
        init = jnp.pad(conv_past, ((0, 0), (HALO - keep, 0), (0, 0)))
    z = _conv_module(glu, init, P["w_dw"], P["b_dw"], P["ln_g"], P["ln_b"], B, T, tr=min(256, T))
    conv_new = glu.reshape(B, T, D)[:, -keep:]
    x, = _matmul(f"conv_pw2_{tag}", [z], [P["w_2"]], D, one, _ep_bias_residual, [F32], tile=[x], col=[P["b_2"]],
                 **wide)
    x = _ffn(x, P, 1, tag)

    return (x.reshape(B, T, D), k32.reshape(1, B, T, H, HEAD_DIM), v32.reshape(1, B, T, H, HEAD_DIM),
            logf[None], s_new[None], conv_new[None])


def kernel(x_prompt, x_sample, cache_k, cache_v, cache_logf, state_ret, state_conv,
           norm_mix, norm_ffn, w_in, b_forget, q_norm_gain, k_norm_gain, ret_norm_gain, w_out,
           w_pw1, b_pw1, w_dw, b_dw, conv_ln_gain, conv_ln_bias, w_pw2, b_pw2,
           w_ffn_gate, w_ffn_up, w_ffn_down):
    P = _prepare_weights(norm_mix, norm_ffn, w_in, b_forget, q_norm_gain, k_norm_gain, ret_norm_gain, w_out,
                         w_pw1, b_pw1, w_dw, b_dw, conv_ln_gain, conv_ln_bias, w_pw2, b_pw2,
                         w_ffn_gate, w_ffn_up, w_ffn_down)
    pos_p = jnp.arange(x_prompt.shape[1])
    y_p, k_p, v_p, lf_p, r_p, c_p = _trunk(x_prompt, pos_p, None, None, P, "p")
    _, Bs, Pl, H, hd = cache_k.shape
    pos_s = Pl + jnp.arange(x_sample.shape[1])
    past = (cache_k.reshape(Bs, Pl, H * hd), cache_v.reshape(Bs, Pl, H * hd), cache_logf.reshape(Bs, Pl, H),
            state_ret.reshape(Bs, H, hd, hd))
    conv_past = state_conv.reshape(state_conv.shape[1:])
    y_s, k_s, v_s, lf_s, r_s, c_s = _trunk(x_sample, pos_s, past, conv_past, P, "s")
    return (y_p, y_s, k_p, v_p, lf_p, r_p, c_p, k_s, v_s, lf_s, r_s, c_s)
```

```python
import functools
import math
from typing import NamedTuple, Optional

import jax
import jax.numpy as jnp
from jax import lax
from jax.experimental import pallas as pl
from jax.experimental.pallas import tpu as pltpu

F32 = jnp.float32
BF16 = jnp.bfloat16

LANES = 128
SUBLANES = 8
HALO = 32
VMEM_LIMIT = 56 * 1024 * 1024

HEAD_DIM = 128
N_HEADS = 8
WIDTH = N_HEADS * HEAD_DIM
CONV_WIDTH = 31
ROPE_BASE = 10000.0
EPS = 1e-6
NEG = -1e30
LOG2E = math.log2(math.e)
CONV_SPAN = (CONV_WIDTH - 1) // SUBLANES * SUBLANES
RET_CHUNK = 256
RET_UNROLL = 4


def _params(n_grid):
    return pltpu.CompilerParams(dimension_semantics=("arbitrary",) * n_grid, vmem_limit_bytes=VMEM_LIMIT)


def _silu(x):
    return x * jax.nn.sigmoid(x)


def _dot(a, b):
    return jnp.dot(a, b, preferred_element_type=F32)


def _dot_nt(a, b):
    return lax.dot_general(a, b, (((1,), (1,)), ((), ())), preferred_element_type=F32)


def _rms_kernel(x_ref, g_ref, o_ref):
    x = x_ref[...]
    ms = jnp.mean(x * x, axis=-1, keepdims=True)
    o_ref[...] = (x * lax.rsqrt(ms + EPS) * g_ref[...]).astype(o_ref.dtype)


def _rmsnorm(x, g, name, tm=512):
    M, D = x.shape
    return pl.pallas_call(
        _rms_kernel,
        grid=(M // tm,),
        in_specs=[pl.BlockSpec((tm, D), lambda i: (i, 0)), pl.BlockSpec((1, D), lambda i: (0, 0))],
        out_specs=pl.BlockSpec((tm, D), lambda i: (i, 0)),
        out_shape=jax.ShapeDtypeStruct((M, D), BF16),
        compiler_params=_params(1),
        name=name,
    )(x, g.reshape(1, D))


class _Rhs(NamedTuple):
    arr: jax.Array
    lead: Optional[int]
    k: int
    k_blk: int = 0
    col: int = 0


def _mm_kernel(*refs, n_lhs, n_rhs, n_extra, n_out, products, epilogue):
    lhs = refs[:n_lhs]
    rhs = refs[n_lhs:n_lhs + n_rhs]
    extras = refs[n_lhs + n_rhs:n_lhs + n_rhs + n_extra]
    outs = refs[n_lhs + n_rhs + n_extra:n_lhs + n_rhs + n_extra + n_out]
    wbuf = refs[n_lhs + n_rhs + n_extra + n_out:]

    @pl.when(pl.program_id(1) == 0)
    def _():
        for w, b in zip(rhs, wbuf):
            b[...] = w[...].astype(BF16)

    accs = []
    for prod in products:
        acc = None
        for a, b in prod:
            d = _dot(lhs[a][...], wbuf[b][...])
            acc = d if acc is None else acc + d
        accs.append(acc)
    epilogue(accs, extras, outs)


def _matmul(name, lhs, rhs, n_cols, products, epilogue, out_dtypes, *, tile=(), col=(), row=(), full=(),
            tm=512, tn=512):
    M = lhs[0].shape[0]
    assert M % tm == 0 and n_cols % tn == 0
    in_specs = [pl.BlockSpec((tm, a.shape[1]), lambda j, i: (i, 0)) for a in lhs]
    for w in rhs:
        assert w.col % tn == 0
        c0 = w.col // tn
        if w.lead is None:
            in_specs.append(pl.BlockSpec((w.k, tn), lambda j, i, kb=w.k_blk, c0=c0: (kb, j + c0)))
        else:
            in_specs.append(pl.BlockSpec((None, w.k, tn),
                                         lambda j, i, ld=w.lead, kb=w.k_blk, c0=c0: (ld, kb, j + c0)))
    in_specs += [pl.BlockSpec((tm, tn), lambda j, i: (i, j)) for _ in tile]
    in_specs += [pl.BlockSpec((1, tn), lambda j, i: (0, j)) for _ in col]
    for r in row:
        assert r.shape[0] % tm == 0
        in_specs.append(pl.BlockSpec((tm, r.shape[1]), lambda j, i, nblk=r.shape[0] // tm: (i % nblk, 0)))
    in_specs += [pl.BlockSpec(f.shape, lambda j, i, nd=f.ndim: (0,) * nd) for f in full]
    extras = tuple(tile) + tuple(col) + tuple(row) + tuple(full)
    kern = functools.partial(_mm_kernel, n_lhs=len(lhs), n_rhs=len(rhs), n_extra=len(extras),
                             n_out=len(out_dtypes), products=products, epilogue=epilogue)
    return pl.pallas_call(
        kern,
        grid=(n_cols // tn, M // tm),
        in_specs=in_specs,
        out_specs=[pl.BlockSpec((tm, tn), lambda j, i: (i, j)) for _ in out_dtypes],
        out_shape=[jax.ShapeDtypeStruct((M, n_cols), dt) for dt in out_dtypes],
        scratch_shapes=[pltpu.VMEM((w.k, tn), BF16) for w in rhs],
        compiler_params=_params(2),
        name=name,
    )(*lhs, *[w.arr for w in rhs], *extras)


def _ep_headnorm(accs, extras, outs, *, scales):
    gain = extras[0][...]
    y = accs[0]
    for h in range(y.shape[1] // HEAD_DIM):
        sl = slice(h * HEAD_DIM, (h + 1) * HEAD_DIM)
        yh = y[:, sl]
        ms = jnp.mean(yh * yh, axis=-1, keepdims=True)
        r = yh * lax.rsqrt(ms + EPS) * gain
        for o, sc in zip(outs, scales):
            o[:, sl] = (r if sc == 1.0 else r * sc).astype(o.dtype)


def _ep_identity(accs, extras, outs):
    for o in outs:
        o[...] = accs[0].astype(o.dtype)


def _ep_logsigmoid(accs, extras, outs):
    z = accs[0] + extras[0][...]
    outs[0][...] = -(jnp.maximum(-z, 0.0) + jnp.log1p(jnp.exp(-jnp.abs(z))))


def _ep_rope(accs, extras, outs):
    scale, cos, sin = extras[0][...], extras[1][...], extras[2][...]
    y = accs[0]
    for h in range(y.shape[1] // HEAD_DIM):
        sl = slice(h * HEAD_DIM, (h + 1) * HEAD_DIM)
        yh = y[:, sl]
        r = yh * cos + pltpu.roll(yh, HEAD_DIM // 2, 1) * sin
        outs[0][:, sl] = (r * scale[:, sl]).astype(outs[0].dtype)


def _ep_residual(accs, extras, outs):
    outs[0][...] = extras[0][...] + accs[0]


def _ep_bias_residual(accs, extras, outs):
    outs[0][...] = extras[0][...] + (accs[0] + extras[1][...])


def _ep_swiglu(accs, extras, outs):
    outs[0][...] = (_silu(accs[0]) * accs[1]).astype(outs[0].dtype)


def _ep_glu(accs, extras, outs):
    a = accs[0] + extras[0][...]
    g = accs[1] + extras[1][...]
    outs[0][...] = a * jax.nn.sigmoid(g)


def _split3(x):
    hi = x.astype(BF16)
    r1 = x - hi.astype(F32)
    mid = r1.astype(BF16)
    lo = (r1 - mid.astype(F32)).astype(BF16)
    return hi, mid, lo


def _cumsum_kernel(x_ref, c_ref, hi_ref, mid_ref, lo_ref, *, scale):
    nblk, R, _ = x_ref.shape
    row = lax.broadcasted_iota(jnp.int32, (LANES, LANES), 0)
    col = lax.broadcasted_iota(jnp.int32, (LANES, LANES), 1)
    tri = jnp.where(row <= col, 1.0, 0.0).astype(BF16)
    ones = jnp.ones((LANES, LANES), BF16)

    def body(b, carry):
        hi, mid, lo = _split3(x_ref[b])
        within = _dot(hi, tri) + _dot(mid, tri) + _dot(lo, tri)
        total = _dot(hi, ones) + _dot(mid, ones) + _dot(lo, ones)
        c = (carry + within) * scale
        c_ref[b] = c
        hi_ref[b], mid_ref[b], lo_ref[b] = _split3(c)
        return carry + total

    lax.fori_loop(0, nblk, body, jnp.zeros((R, LANES), F32))


def _cumsum_rows(x, name, scale):
    R, N = x.shape
    nblk = N // LANES
    x3 = x.reshape(R, nblk, LANES).transpose(1, 0, 2)
    outs = pl.pallas_call(
        functools.partial(_cumsum_kernel, scale=scale),
        out_shape=[jax.ShapeDtypeStruct((nblk, R, LANES), dt) for dt in (F32, BF16, BF16, BF16)],
        compiler_params=pltpu.CompilerParams(vmem_limit_bytes=VMEM_LIMIT),
        name=name,
    )(x3)
    c, hi, mid, lo = [o.transpose(1, 0, 2).reshape(R, N) for o in outs]
    return c, (hi, mid, lo)


def _flash_step_t(st, vt, carry, mask):
    m, l, acc = carry
    if mask is not None:
        st = jnp.where(mask, st, NEG)
    m_new = jnp.maximum(m, jnp.max(st, axis=0, keepdims=True))
    alpha = jnp.exp2(m - m_new)
    pt = jnp.exp2(st - m_new)
    l = alpha * l + jnp.sum(pt, axis=0, keepdims=True)
    acc = alpha * acc + _dot(vt, pt.astype(BF16))
    return m_new, l, acc


def _flash_init_t(tq):
    return (jnp.full((1, tq), NEG, F32), jnp.zeros((1, tq), F32), jnp.zeros((HEAD_DIM, tq), F32))


def _causal_mask_t(tk, tq, q_off):
    keys = lax.broadcasted_iota(jnp.int32, (tk, tq), 0)
    queries = lax.broadcasted_iota(jnp.int32, (tk, tq), 1) + q_off
    return keys <= queries


def _fox_prompt_kernel(q_ref, qb_ref, k_ref, kb_ref, vt_ref, o_ref, *, tq, hb, qsplit):
    i = pl.program_id(2)
    tqc = tq // qsplit
    heads = [slice(h * HEAD_DIM, (h + 1) * HEAD_DIM) for h in range(hb)]
    chains = [(h, slice(c * tqc, (c + 1) * tqc)) for h in range(hb) for c in range(qsplit)]
    qs = [jnp.concatenate([q_ref[qr, heads[h]], qb_ref[h, qr, :]], axis=1) for h, qr in chains]

    def block(j, carries, diagonal):
        rows = pl.ds(pl.multiple_of(j * tq, tq), tq)
        ks = [jnp.concatenate([k_ref[rows, hs], kb_ref[h, rows, :]], axis=1) for h, hs in enumerate(heads)]
        sts = [_dot_nt(ks[h], qs[n]) for n, (h, _) in enumerate(chains)]
        return tuple(
            _flash_step_t(sts[n], vt_ref[h, j], carries[n], _causal_mask_t(tq, tqc, qr.start) if diagonal else None)
            for n, (h, qr) in enumerate(chains))

    carries = lax.fori_loop(0, i, lambda j, c: block(j, c, False), (_flash_init_t(tqc),) * len(chains))
    carries = block(i, carries, True)
    for (m, l, acc), (h, qr) in zip(carries, chains):
        o_ref[qr, heads[h]] = (acc / l).T.astype(o_ref.dtype)


def _bias_columns(pieces):
    parts = jnp.stack(pieces, axis=-1)
    ones = jnp.ones_like(parts)
    pad = ((0, 0), (0, 0), (0, LANES - 6))
    return (jnp.pad(jnp.concatenate([parts, ones], axis=-1), pad),
            jnp.pad(jnp.concatenate([ones, -parts], axis=-1), pad))


def _fox_prompt(q, k, v, c2_pieces, B, S, tq=512, hb=2, qsplit=2):
    H = N_HEADS
    nq = S // tq
    nh = H // hb
    qb, kb = _bias_columns(c2_pieces)
    vt = v.reshape(B, nq, tq, H, HEAD_DIM).transpose(0, 3, 1, 4, 2).reshape(B * H, nq, HEAD_DIM, tq)
    return pl.pallas_call(
        functools.partial(_fox_prompt_kernel, tq=tq, hb=hb, qsplit=qsplit),
        grid=(B, nh, nq),
        in_specs=[
            pl.BlockSpec((tq, hb * HEAD_DIM), lambda b, h, i: (b * nq + i, h)),
            pl.BlockSpec((hb, tq, LANES), lambda b, h, i: (b * nh + h, i, 0)),
            pl.BlockSpec((S, hb * HEAD_DIM), lambda b, h, i: (b, h)),
            pl.BlockSpec((hb, S, LANES), lambda b, h, i: (b * nh + h, 0, 0)),
            pl.BlockSpec((hb, nq, HEAD_DIM, tq), lambda b, h, i: (b * nh + h, 0, 0, 0)),
        ],
        out_specs=pl.BlockSpec((tq, hb * HEAD_DIM), lambda b, h, i: (b * nq + i, h)),
        out_shape=jax.ShapeDtypeStruct((B * S, WIDTH), BF16),
        compiler_params=_params(3),
        name="fox_prompt",
    )(q, qb, k, kb, vt)


def _causal_mask(tq, tk):
    rows = lax.broadcasted_iota(jnp.int32, (tq, tk), 0)
    cols = lax.broadcasted_iota(jnp.int32, (tq, tk), 1)
    return cols <= rows


def _softmax_update(s, m, l):
    m_new = jnp.maximum(m, jnp.max(s, axis=-1, keepdims=True))
    alpha = jnp.exp2(m - m_new)
    p = jnp.exp2(s - m_new)
    return m_new, alpha * l + jnp.sum(p, axis=-1, keepdims=True), alpha, p


def _fox_sample_kernel(q_ref, kc_ref, vc_ref, kn_ref, vn_ref, cq_ref, ckc_ref, ckn_ref, o_ref, m_scr, l_scr, acc_scr):
    j = pl.program_id(1)
    T = q_ref.shape[0]
    heads = [slice(h * HEAD_DIM, (h + 1) * HEAD_DIM) for h in range(N_HEADS)]

    @pl.when(j == 0)
    def _():
        m_scr[...] = jnp.full(m_scr.shape, NEG, F32)
        l_scr[...] = jnp.zeros(l_scr.shape, F32)
        acc_scr[...] = jnp.zeros(acc_scr.shape, F32)

    k_all = pltpu.einshape("mhd->hmd", kc_ref[...])
    v_all = pltpu.einshape("mhd->hmd", vc_ref[...])
    for h, hs in enumerate(heads):
        s = _dot_nt(q_ref[:, hs], k_all[h].astype(BF16)) + (cq_ref[h] - ckc_ref[h])
        m, l, alpha, p = _softmax_update(s, m_scr[h], l_scr[h])
        acc_scr[h] = alpha * acc_scr[h] + _dot(p.astype(BF16), v_all[h].astype(BF16))
        m_scr[h], l_scr[h] = m, l

    @pl.when(j == pl.num_programs(1) - 1)
    def _():
        mask = _causal_mask(T, T)
        for h, hs in enumerate(heads):
            s = jnp.where(mask, _dot_nt(q_ref[:, hs], kn_ref[:, hs]) + (cq_ref[h] - ckn_ref[h]), NEG)
            _, l, alpha, p = _softmax_update(s, m_scr[h], l_scr[h])
            acc = alpha * acc_scr[h] + _dot(p.astype(BF16), vn_ref[:, hs])
            o_ref[:, hs] = (acc / l).astype(o_ref.dtype)


def _fox_sample(q, k_new, v_new, cache_k, cache_v, c2, B, T, tk=512):
    H = N_HEADS
    P = cache_k.shape[2]
    nk = P // tk
    c_q = c2[:, P:].reshape(B, H, T, 1)
    c_kc = c2[:, :P].reshape(B, H, nk, 1, tk)
    c_kn = c2[:, P:].reshape(B, H, 1, T)
    tok = pl.BlockSpec((T, WIDTH), lambda b, j: (b, 0))
    cache = pl.BlockSpec((None, None, tk, H, HEAD_DIM), lambda b, j: (0, b, j, 0, 0))
    return pl.pallas_call(
        _fox_sample_kernel,
        grid=(B, nk),
        in_specs=[
            tok, cache, cache, tok, tok,
            pl.BlockSpec((None, H, T, 1), lambda b, j: (b, 0, 0, 0)),
            pl.BlockSpec((None, H, None, 1, tk), lambda b, j: (b, 0, j, 0, 0)),
            pl.BlockSpec((None, H, 1, T), lambda b, j: (b, 0, 0, 0)),
        ],
        out_specs=tok,
        out_shape=jax.ShapeDtypeStruct((B * T, WIDTH), BF16),
        scratch_shapes=[pltpu.VMEM((H, T, 1), F32), pltpu.VMEM((H, T, 1), F32), pltpu.VMEM((H, T, HEAD_DIM), F32)],
        compiler_params=_params(2),
        name="fox_sample",
    )(q, cache_k, cache_v, k_new, v_new, c_q, c_kc, c_kn)


def _retention_kernel(q_ref, k_ref, v_ref, g_ref, gn_ref, s0_ref, dm_ref, qd_ref, kd_ref, cd_ref,
                      o_ref, s_ref, *, L, hb, unroll):
    T = q_ref.shape[0]

    def chunk(off, h, s):
        rows = pl.ds(off, L)
        hs = slice(h * HEAD_DIM, (h + 1) * HEAD_DIM)
        q = q_ref[rows, hs]
        k = k_ref[rows, hs]
        v = v_ref[rows, hs].astype(BF16)
        att = _dot_nt(q, k) * dm_ref[h]
        o = _dot(att.astype(BF16), v) + _dot((q.astype(F32) * qd_ref[h]).astype(BF16), s.astype(BF16))
        kd = k.astype(F32) * kd_ref[h]
        s_new = s * cd_ref[h] + _dot(kd.T.astype(BF16), v)
        mu = jnp.mean(o, axis=-1, keepdims=True)
        oc = o - mu
        y = oc * lax.rsqrt(jnp.mean(oc * oc, axis=-1, keepdims=True) + EPS)
        o_ref[rows, hs] = (y * gn_ref[:, hs] * _silu(g_ref[rows, hs])).astype(o_ref.dtype)
        return s_new

    def body(it, states):
        states = list(states)
        for u in range(unroll):
            off = pl.multiple_of((it * unroll + u) * L, L)
            for h in range(hb):
                states[h] = chunk(off, h, states[h])
        return tuple(states)

    states = lax.fori_loop(0, T // (L * unroll), body, tuple(s0_ref[h] for h in range(hb)))
    for h in range(hb):
        s_ref[h] = states[h]


def _retention_consts(L):
    H = N_HEADS
    log_g = jnp.log1p(-jnp.exp2(-5.0 - jnp.arange(H, dtype=F32)))
    i = jnp.arange(L, dtype=F32)
    diff = i[:, None] - i[None, :]
    dmat = jnp.where(diff[None] >= 0, jnp.exp(jnp.maximum(diff, 0.0)[None] * log_g[:, None, None]), 0.0)
    q_dec = jnp.exp((i + 1.0)[None, :] * log_g[:, None])
    k_dec = jnp.exp((L - 1.0 - i)[None, :] * log_g[:, None])
    c_dec = jnp.exp(L * log_g)
    rep = lambda a: jnp.broadcast_to(a[..., None], a.shape + (LANES,))
    return dmat, rep(q_dec), rep(k_dec), rep(c_dec[:, None])


def _retention(qk, vg, gn, s0, B, T):
    H = N_HEADS
    L = min(RET_CHUNK, T)
    n_chunks = T // L
    unroll = math.gcd(RET_UNROLL, n_chunks)
    hb = 1 if n_chunks > 1 else H
    nh = H // hb
    dmat, q_dec, k_dec, c_dec = _retention_consts(L)
    seq = lambda off: pl.BlockSpec((T, hb * HEAD_DIM), lambda b, h: (b, h + off))
    per_head = lambda r: pl.BlockSpec((hb, r, LANES), lambda b, h: (h, 0, 0))
    state = pl.BlockSpec((None, hb, HEAD_DIM, HEAD_DIM), lambda b, h: (b, h, 0, 0))
    return pl.pallas_call(
        functools.partial(_retention_kernel, L=L, hb=hb, unroll=unroll),
        grid=(B, nh),
        in_specs=[seq(0), seq(nh), seq(0), seq(nh),
                  pl.BlockSpec((1, hb * HEAD_DIM), lambda b, h: (0, h)), state,
                  pl.BlockSpec((hb, L, L), lambda b, h: (h, 0, 0)), per_head(L), per_head(L), per_head(1)],
        out_specs=[seq(0), state],
        out_shape=[jax.ShapeDtypeStruct((B * T, WIDTH), BF16), jax.ShapeDtypeStruct((B, H, HEAD_DIM, HEAD_DIM), F32)],
        compiler_params=_params(2),
        name="retention",
    )(qk, qk, vg, vg, gn.reshape(1, WIDTH), s0, dmat, q_dec, k_dec, c_dec)


def _conv_kernel(x_ref, prev_ref, init_ref, w_ref, b_ref, g_ref, beta_ref, o_ref, xp_ref, xs_ref, z_ref, *, tr):
    i = pl.program_id(1)

    @pl.when(i == 0)
    def _():
        xp_ref[0:HALO, :] = init_ref[...]

    @pl.when(i > 0)
    def _():
        xp_ref[0:HALO, :] = prev_ref[...]

    C = x_ref.shape[1]
    xp_ref[HALO:HALO + tr, :] = x_ref[...]
    xp_ref[HALO + tr:HALO + tr + SUBLANES, :] = jnp.zeros((SUBLANES, C), F32)
    first = HALO - (CONV_WIDTH - 1)
    span = tr + CONV_SPAN
    for r in range(SUBLANES):
        xs_ref[r] = xp_ref[pl.ds(first + r, span), :]

    groups = CONV_SPAN // SUBLANES + 1
    for c in range(C // LANES):
        cs = slice(c * LANES, (c + 1) * LANES)
        wts = [w_ref[w, :, cs] for w in range(CONV_WIDTH)]
        bias = jnp.broadcast_to(b_ref[:, cs], (SUBLANES, LANES))

        def step(u, accs, cs=cs, wts=wts, bias=bias):
            accs = (bias,) + accs
            r0 = pl.multiple_of(u * SUBLANES, SUBLANES)
            for r in range(SUBLANES):
                x = xs_ref[r, pl.ds(r0, SUBLANES), cs]
                accs = tuple(acc + x * wts[SUBLANES * a + r] if SUBLANES * a + r < CONV_WIDTH else acc
                             for a, acc in enumerate(accs))
            z_ref[pl.ds(r0, SUBLANES), cs] = accs[-1]
            return accs[:-1]

        lax.fori_loop(0, span // SUBLANES, step, (bias,) * (groups - 1))

    z = z_ref[CONV_SPAN:CONV_SPAN + tr, :]
    mu = jnp.mean(z, axis=-1, keepdims=True)
    zc = z - mu
    y = zc * lax.rsqrt(jnp.mean(zc * zc, axis=-1, keepdims=True) + EPS) * g_ref[...] + beta_ref[...]
    o_ref[...] = _silu(y).astype(o_ref.dtype)


def _conv_module(glu, init, w_dw, b_dw, ln_g, ln_b, B, T, tr):
    C = glu.shape[1]
    nt = T // tr
    per = tr // HALO
    w_rep = jnp.broadcast_to(jnp.pad(w_dw, ((0, HALO - CONV_WIDTH), (0, 0)))[:, None, :], (HALO, SUBLANES, C))
    vec = pl.BlockSpec((1, C), lambda b, i: (0, 0))
    return pl.pallas_call(
        functools.partial(_conv_kernel, tr=tr),
        grid=(B, nt),
        in_specs=[
            pl.BlockSpec((tr, C), lambda b, i: (b * nt + i, 0)),
            pl.BlockSpec((HALO, C), lambda b, i: (b * nt * per + jnp.maximum(i * per - 1, 0), 0)),
            pl.BlockSpec((None, HALO, C), lambda b, i: (b, 0, 0)),
            pl.BlockSpec((HALO, SUBLANES, C), lambda b, i: (0, 0, 0)),
            vec, vec, vec,
        ],
        out_specs=pl.BlockSpec((tr, C), lambda b, i: (b * nt + i, 0)),
        out_shape=jax.ShapeDtypeStruct((B * T, C), BF16),
        scratch_shapes=[pltpu.VMEM((HALO + tr + SUBLANES, C), F32),
                        pltpu.VMEM((SUBLANES, tr + CONV_SPAN, C), F32),
                        pltpu.VMEM((tr + CONV_SPAN, C), F32)],
        compiler_params=_params(2),
        name="conv_module",
    )(glu, glu, init, w_rep, b_dw.reshape(1, C), ln_g.reshape(1, C), ln_b.reshape(1, C))


def _rope_tables(pos):
    half = HEAD_DIM // 2
    inv = jnp.exp(-math.log(ROPE_BASE) * jnp.arange(half, dtype=F32) / half)
    ang = pos.astype(F32)[:, None] * inv[None, :]
    cos, sin = jnp.cos(ang), jnp.sin(ang)
    return jnp.concatenate([cos, cos], axis=-1), jnp.concatenate([-sin, sin], axis=-1)


def _prepare_weights(norm_mix, norm_ffn, w_in, b_forget, q_norm_gain, k_norm_gain, ret_norm_gain, w_out,
                     w_pw1, b_pw1, w_dw, b_dw, conv_ln_gain, conv_ln_bias, w_pw2, b_pw2,
                     w_ffn_gate, w_ffn_up, w_ffn_down):
    W = WIDTH
    D = w_in.shape[1]
    f0 = 3 * W
    f1 = f0 + N_HEADS
    w_b = w_in[0, :, f1:]
    return dict(
        norm_mix=norm_mix, norm_ffn=norm_ffn,
        w_q=_Rhs(w_in, 0, D, col=0), w_k=_Rhs(w_in, 0, D, col=W), w_v=_Rhs(w_in, 0, D, col=2 * W),
        w_f=_Rhs(w_in, 0, D, col=f0),
        b_f=jnp.pad(b_forget[0], (0, LANES - N_HEADS)).reshape(1, LANES),
        w_qkb=_Rhs(w_b, None, D, col=0), w_vgb=_Rhs(w_b, None, D, col=2 * W),
        rope_scale=jnp.concatenate([jnp.ones((1, W), F32), jnp.full((1, W), HEAD_DIM ** -0.5, F32)], axis=1),
        gq=q_norm_gain[0].reshape(1, HEAD_DIM), gk=k_norm_gain[0].reshape(1, HEAD_DIM), gn=ret_norm_gain[0],
        w_oa=_Rhs(w_out, 0, W, k_blk=0), w_ob=_Rhs(w_out, 0, W, k_blk=1),
        w_1a=_Rhs(w_pw1, 0, D, col=0), w_1g=_Rhs(w_pw1, 0, D, col=w_pw1.shape[2] // 2),
        b_1a=b_pw1[0, :2 * W].reshape(1, -1), b_1g=b_pw1[0, 2 * W:].reshape(1, -1),
        w_dw=w_dw[0], b_dw=b_dw[0], ln_g=conv_ln_gain[0], ln_b=conv_ln_bias[0],
        w_2=_Rhs(w_pw2, 0, w_pw2.shape[1]), b_2=b_pw2[0].reshape(1, -1),
        w_g=[_Rhs(w_ffn_gate, l, D) for l in range(2)], w_u=[_Rhs(w_ffn_up, l, D) for l in range(2)],
        w_d=[_Rhs(w_ffn_down, l, w_ffn_down.shape[1]) for l in range(2)],
        d_ff=w_ffn_gate.shape[2],
    )


def _ffn(x, P, layer, tag):
    D = x.shape[1]
    h = _rmsnorm(x, P["norm_ffn"][layer], f"rms_ffn{layer}_{tag}")
    act, = _matmul(f"ffn_up{layer}_{tag}", [h], [P["w_g"][layer], P["w_u"][layer]], P["d_ff"],
                   [[(0, 0)], [(0, 1)]], _ep_swiglu, [BF16])
    out, = _matmul(f"ffn_down{layer}_{tag}", [act], [P["w_d"][layer]], D, [[(0, 0)]], _ep_residual, [F32],
                   tile=[x])
    return out


def _trunk(x3, pos, past, conv_past, P, tag):
    B, T, D = x3.shape
    M = B * T
    H = N_HEADS
    W = WIDTH
    x = x3.reshape(M, D)
    one = [[(0, 0)]]
    wide = dict(tn=1024)

    h = _rmsnorm(x, P["norm_mix"][0], f"rms_mix0_{tag}")
    q, = _matmul(f"proj_q_{tag}", [h], [P["w_q"]], W, one,
                 functools.partial(_ep_headnorm, scales=(HEAD_DIM ** -0.5 * LOG2E,)), [BF16], full=[P["gq"]], **wide)
    k32, k16 = _matmul(f"proj_k_{tag}", [h], [P["w_k"]], W, one,
                       functools.partial(_ep_headnorm, scales=(1.0, 1.0)), [F32, BF16], full=[P["gk"]], **wide)
    v32, v16 = _matmul(f"proj_v_{tag}", [h], [P["w_v"]], W, one, _ep_identity, [F32, BF16], **wide)
    logf_pad, = _matmul(f"proj_f_{tag}", [h], [P["w_f"]], LANES, one, _ep_logsigmoid, [F32], col=[P["b_f"]],
                        tn=LANES)
    cos, sin = _rope_tables(pos)
    reps = max(1, 512 // T)
    cos, sin = jnp.tile(cos, (reps, 1)), jnp.tile(sin, (reps, 1))
    qk_b, = _matmul(f"proj_qkb_{tag}", [h], [P["w_qkb"]], 2 * W, one, _ep_rope, [BF16], col=[P["rope_scale"]],
                    row=[cos, sin], **wide)
    vg_b, = _matmul(f"proj_vgb_{tag}", [h], [P["w_vgb"]], 2 * W, one, _ep_identity, [F32], **wide)

    logf = logf_pad[:, :H].reshape(B, T, H)
    lf_t = logf.transpose(0, 2, 1).reshape(B * H, T)
    if past is None:
        _, c2_pieces = _cumsum_rows(lf_t, f"cumsum_{tag}", LOG2E)
        oa = _fox_prompt(q, k16, v16, c2_pieces, B, T)
        s0 = jnp.zeros((B, H, HEAD_DIM, HEAD_DIM), F32)
    else:
        kc, vc, lfc, s0 = past
        Pl = kc.shape[2]
        lf_all = jnp.concatenate([lfc.transpose(0, 2, 1).reshape(B * H, Pl), lf_t], axis=1)
        pad = (-lf_all.shape[1]) % LANES
        c2, _ = _cumsum_rows(jnp.pad(lf_all, ((0, 0), (0, pad))), f"cumsum_{tag}", LOG2E)
        oa = _fox_sample(q, k16, v16, kc, vc, c2[:, :Pl + T], B, T)
    ob, s_new = _retention(qk_b, vg_b, P["gn"], s0, B, T)
    x, = _matmul(f"out_proj_{tag}", [oa, ob], [P["w_oa"], P["w_ob"]], D, [[(0, 0), (1, 1)]], _ep_residual, [F32],
                 tile=[x], **wide)
    x = _ffn(x, P, 0, tag)

    h = _rmsnorm(x, P["norm_mix"][1], f"rms_mix1_{tag}")
    glu, = _matmul(f"conv_pw1_{tag}", [h], [P["w_1a"], P["w_1g"]], D, [[(0, 0)], [(0, 1)]], _ep_glu, [F32],
                   col=[P["b_1a"], P["b_1g"]])
    keep = CONV_WIDTH - 1
    assert T >= keep
    if conv_past is None:
        init = jnp.zeros((B, HALO, D), F32)
    else:
        init = jnp.pad(conv_past, ((0, 0), (HALO - keep, 0), (0, 0)))
    z = _conv_module(glu, init, P["w_dw"], P["b_dw"], P["ln_g"], P["ln_b"], B, T, tr=min(256, T))
    conv_new = glu.reshape(B, T, D)[:, -keep:]
    x, = _matmul(f"conv_pw2_{tag}", [z], [P["w_2"]], D, one, _ep_bias_residual, [F32], tile=[x], col=[P["b_2"]],
                 **wide)
    x = _ffn(x, P, 1, tag)

    return (x.reshape(B, T, D), k32.reshape(1, B, T, H, HEAD_DIM), v32.reshape(1, B, T, H, HEAD_DIM),
            logf[None], s_new[None], conv_new[None])


def kernel(x_prompt, x_sample, cache_k, cache_v, cache_logf, state_ret, state_conv,
           norm_mix, norm_ffn, w_in, b_forget, q_norm_gain, k_norm_gain, ret_norm_gain, w_out,
           w_pw1, b_pw1, w_dw, b_dw, conv_ln_gain, conv_ln_bias, w_pw2, b_pw2,
           w_ffn_gate, w_ffn_up, w_ffn_down):
    P = _prepare_weights(norm_mix, norm_ffn, w_in, b_forget, q_norm_gain, k_norm_gain, ret_norm_gain, w_out,
                         w_pw1, b_pw1, w_dw, b_dw, conv_ln_gain, conv_ln_bias, w_pw2, b_pw2,
                         w_ffn_gate, w_ffn_up, w_ffn_down)
    pos_p = jnp.arange(x_prompt.shape[1])
    y_p, k_p, v_p, lf_p, r_p, c_p = _trunk(x_prompt, pos_p, None, None, P, "p")
    _, Bs, Pl, H, hd = cache_k.shape
    pos_s = Pl + jnp.arange(x_sample.shape[1])
    past = (cache_k, cache_v, cache_logf.reshape(Bs, Pl, H), state_ret.reshape(Bs, H, hd, hd))
    conv_past = state_conv.reshape(state_conv.shape[1:])
    y_s, k_s, v_s, lf_s, r_s, c_s = _trunk(x_sample, pos_s, past, conv_past, P, "s")
    return (y_p, y_s, k_p, v_p, lf_p, r_p, c_p, k_s, v_s, lf_s, r_s, c_s)
```

```python
import functools
import math
from typing import NamedTuple, Optional

import jax
import jax.numpy as jnp
from jax import lax
from jax.experimental import pallas as pl
from jax.experimental.pallas import tpu as pltpu

F32 = jnp.float32
BF16 = jnp.bfloat16

LANES = 128
SUBLANES = 8
HALO = 32
VMEM_LIMIT = 56 * 1024 * 1024

HEAD_DIM = 128
N_HEADS = 8
WIDTH = N_HEADS * HEAD_DIM
CONV_WIDTH = 31
ROPE_BASE = 10000.0
EPS = 1e-6
NEG = -1e30
LOG2E = math.log2(math.e)
CONV_SPAN = (CONV_WIDTH - 1) // SUBLANES * SUBLANES
CONV_UNROLL = 5
RET_CHUNK = 256
RET_UNROLL = 4


def _params(n_grid):
    return pltpu.CompilerParams(dimension_semantics=("arbitrary",) * n_grid, vmem_limit_bytes=VMEM_LIMIT)


def _silu(x):
    return x * jax.nn.sigmoid(x)


def _dot(a, b):
    return jnp.dot(a, b, preferred_element_type=F32)


def _dot_nt(a, b):
    return lax.dot_general(a, b, (((1,), (1,)), ((), ())), preferred_element_type=F32)


def _rms_kernel(x_ref, g_ref, o_ref):
    x = x_ref[...]
    ms = jnp.mean(x * x, axis=-1, keepdims=True)
    o_ref[...] = (x * lax.rsqrt(ms + EPS) * g_ref[...]).astype(o_ref.dtype)


def _rmsnorm(x, g, name, tm=512):
    M, D = x.shape
    return pl.pallas_call(
        _rms_kernel,
        grid=(M // tm,),
        in_specs=[pl.BlockSpec((tm, D), lambda i: (i, 0)), pl.BlockSpec((1, D), lambda i: (0, 0))],
        out_specs=pl.BlockSpec((tm, D), lambda i: (i, 0)),
        out_shape=jax.ShapeDtypeStruct((M, D), BF16),
        compiler_params=_params(1),
        name=name,
    )(x, g.reshape(1, D))


class _Rhs(NamedTuple):
    arr: jax.Array
    lead: Optional[int]
    k: int
    k_blk: int = 0
    col: int = 0


def _mm_kernel(*refs, n_lhs, n_rhs, n_extra, n_out, products, epilogue):
    lhs = refs[:n_lhs]
    rhs = refs[n_lhs:n_lhs + n_rhs]
    extras = refs[n_lhs + n_rhs:n_lhs + n_rhs + n_extra]
    outs = refs[n_lhs + n_rhs + n_extra:n_lhs + n_rhs + n_extra + n_out]
    wbuf = refs[n_lhs + n_rhs + n_extra + n_out:]

    @pl.when(pl.program_id(1) == 0)
    def _():
        for w, b in zip(rhs, wbuf):
            b[...] = w[...].astype(BF16)

    accs = []
    for prod in products:
        acc = None
        for a, b in prod:
            d = _dot(lhs[a][...], wbuf[b][...])
            acc = d if acc is None else acc + d
        accs.append(acc)
    epilogue(accs, extras, outs)


def _matmul(name, lhs, rhs, n_cols, products, epilogue, out_dtypes, *, tile=(), col=(), row=(), full=(),
            tm=512, tn=512):
    M = lhs[0].shape[0]
    assert M % tm == 0 and n_cols % tn == 0
    in_specs = [pl.BlockSpec((tm, a.shape[1]), lambda j, i: (i, 0)) for a in lhs]
    for w in rhs:
        assert w.col % tn == 0
        c0 = w.col // tn
        if w.lead is None:
            in_specs.append(pl.BlockSpec((w.k, tn), lambda j, i, kb=w.k_blk, c0=c0: (kb, j + c0)))
        else:
            in_specs.append(pl.BlockSpec((None, w.k, tn),
                                         lambda j, i, ld=w.lead, kb=w.k_blk, c0=c0: (ld, kb, j + c0)))
    in_specs += [pl.BlockSpec((tm, tn), lambda j, i: (i, j)) for _ in tile]
    in_specs += [pl.BlockSpec((1, tn), lambda j, i: (0, j)) for _ in col]
    for r in row:
        assert r.shape[0] % tm == 0
        in_specs.append(pl.BlockSpec((tm, r.shape[1]), lambda j, i, nblk=r.shape[0] // tm: (i % nblk, 0)))
    in_specs += [pl.BlockSpec(f.shape, lambda j, i, nd=f.ndim: (0,) * nd) for f in full]
    extras = tuple(tile) + tuple(col) + tuple(row) + tuple(full)
    kern = functools.partial(_mm_kernel, n_lhs=len(lhs), n_rhs=len(rhs), n_extra=len(extras),
                             n_out=len(out_dtypes), products=products, epilogue=epilogue)
    return pl.pallas_call(
        kern,
        grid=(n_cols // tn, M // tm),
        in_specs=in_specs,
        out_specs=[pl.BlockSpec((tm, tn), lambda j, i: (i, j)) for _ in out_dtypes],
        out_shape=[jax.ShapeDtypeStruct((M, n_cols), dt) for dt in out_dtypes],
        scratch_shapes=[pltpu.VMEM((w.k, tn), BF16) for w in rhs],
        compiler_params=_params(2),
        name=name,
    )(*lhs, *[w.arr for w in rhs], *extras)


def _ep_headnorm(accs, extras, outs, *, scales):
    gain = extras[0][...]
    y = accs[0]
    for h in range(y.shape[1] // HEAD_DIM):
        sl = slice(h * HEAD_DIM, (h + 1) * HEAD_DIM)
        yh = y[:, sl]
        ms = jnp.mean(yh * yh, axis=-1, keepdims=True)
        r = yh * lax.rsqrt(ms + EPS) * gain
        for o, sc in zip(outs, scales):
            o[:, sl] = (r if sc == 1.0 else r * sc).astype(o.dtype)


def _ep_identity(accs, extras, outs):
    for o in outs:
        o[...] = accs[0].astype(o.dtype)


def _ep_logsigmoid(accs, extras, outs):
    z = accs[0] + extras[0][...]
    outs[0][...] = -(jnp.maximum(-z, 0.0) + jnp.log1p(jnp.exp(-jnp.abs(z))))


def _ep_rope(accs, extras, outs):
    scale, cos, sin = extras[0][...], extras[1][...], extras[2][...]
    y = accs[0]
    for h in range(y.shape[1] // HEAD_DIM):
        sl = slice(h * HEAD_DIM, (h + 1) * HEAD_DIM)
        yh = y[:, sl]
        r = yh * cos + pltpu.roll(yh, HEAD_DIM // 2, 1) * sin
        outs[0][:, sl] = (r * scale[:, sl]).astype(outs[0].dtype)


def _ep_residual(accs, extras, outs):
    outs[0][...] = extras[0][...] + accs[0]


def _ep_bias_residual(accs, extras, outs):
    outs[0][...] = extras[0][...] + (accs[0] + extras[1][...])


def _ep_swiglu(accs, extras, outs):
    outs[0][...] = (_silu(accs[0]) * accs[1]).astype(outs[0].dtype)


def _ep_glu(accs, extras, outs):
    a = accs[0] + extras[0][...]
    g = accs[1] + extras[1][...]
    outs[0][...] = a * jax.nn.sigmoid(g)


def _split3(x):
    hi = x.astype(BF16)
    r1 = x - hi.astype(F32)
    mid = r1.astype(BF16)
    lo = (r1 - mid.astype(F32)).astype(BF16)
    return hi, mid, lo


def _cumsum_kernel(x_ref, c_ref, hi_ref, mid_ref, lo_ref, *, scale):
    nblk, R, _ = x_ref.shape
    row = lax.broadcasted_iota(jnp.int32, (LANES, LANES), 0)
    col = lax.broadcasted_iota(jnp.int32, (LANES, LANES), 1)
    tri = jnp.where(row <= col, 1.0, 0.0).astype(BF16)
    ones = jnp.ones((LANES, LANES), BF16)

    def body(b, carry):
        hi, mid, lo = _split3(x_ref[b])
        within = _dot(hi, tri) + _dot(mid, tri) + _dot(lo, tri)
        total = _dot(hi, ones) + _dot(mid, ones) + _dot(lo, ones)
        c = (carry + within) * scale
        c_ref[b] = c
        hi_ref[b], mid_ref[b], lo_ref[b] = _split3(c)
        return carry + total

    lax.fori_loop(0, nblk, body, jnp.zeros((R, LANES), F32))


def _cumsum_rows(x, name, scale):
    R, N = x.shape
    nblk = N // LANES
    x3 = x.reshape(R, nblk, LANES).transpose(1, 0, 2)
    outs = pl.pallas_call(
        functools.partial(_cumsum_kernel, scale=scale),
        out_shape=[jax.ShapeDtypeStruct((nblk, R, LANES), dt) for dt in (F32, BF16, BF16, BF16)],
        compiler_params=pltpu.CompilerParams(vmem_limit_bytes=VMEM_LIMIT),
        name=name,
    )(x3)
    c, hi, mid, lo = [o.transpose(1, 0, 2).reshape(R, N) for o in outs]
    return c, (hi, mid, lo)


def _flash_step_t(st, vt, carry, mask):
    m, l, acc = carry
    if mask is not None:
        st = jnp.where(mask, st, NEG)
    m_new = jnp.maximum(m, jnp.max(st, axis=0, keepdims=True))
    alpha = jnp.exp2(m - m_new)
    pt = jnp.exp2(st - m_new)
    l = alpha * l + jnp.sum(pt, axis=0, keepdims=True)
    acc = alpha * acc + _dot(vt, pt.astype(BF16))
    return m_new, l, acc


def _flash_init_t(tq):
    return (jnp.full((1, tq), NEG, F32), jnp.zeros((1, tq), F32), jnp.zeros((HEAD_DIM, tq), F32))


def _causal_mask_t(tk, tq, q_off):
    keys = lax.broadcasted_iota(jnp.int32, (tk, tq), 0)
    queries = lax.broadcasted_iota(jnp.int32, (tk, tq), 1) + q_off
    return keys <= queries


def _fox_prompt_kernel(q_ref, qb_ref, k_ref, kb_ref, vt_ref, o_ref, *, tq, hb, qsplit):
    i = pl.program_id(2)
    tqc = tq // qsplit
    heads = [slice(h * HEAD_DIM, (h + 1) * HEAD_DIM) for h in range(hb)]
    chains = [(h, slice(c * tqc, (c + 1) * tqc)) for h in range(hb) for c in range(qsplit)]
    qs = [jnp.concatenate([q_ref[qr, heads[h]], qb_ref[h, qr, :]], axis=1) for h, qr in chains]

    def block(j, carries, diagonal):
        rows = pl.ds(pl.multiple_of(j * tq, tq), tq)
        ks = [jnp.concatenate([k_ref[rows, hs], kb_ref[h, rows, :]], axis=1) for h, hs in enumerate(heads)]
        sts = [_dot_nt(ks[h], qs[n]) for n, (h, _) in enumerate(chains)]
        return tuple(
            _flash_step_t(sts[n], vt_ref[h, j], carries[n], _causal_mask_t(tq, tqc, qr.start) if diagonal else None)
            for n, (h, qr) in enumerate(chains))

    carries = lax.fori_loop(0, i, lambda j, c: block(j, c, False), (_flash_init_t(tqc),) * len(chains))
    carries = block(i, carries, True)
    for (m, l, acc), (h, qr) in zip(carries, chains):
        o_ref[qr, heads[h]] = (acc / l).T.astype(o_ref.dtype)


def _bias_columns(pieces):
    parts = jnp.stack(pieces, axis=-1)
    ones = jnp.ones_like(parts)
    pad = ((0, 0), (0, 0), (0, LANES - 6))
    return (jnp.pad(jnp.concatenate([parts, ones], axis=-1), pad),
            jnp.pad(jnp.concatenate([ones, -parts], axis=-1), pad))


def _fox_prompt(q, k, v, c2_pieces, B, S, tq=512, hb=2, qsplit=2):
    H = N_HEADS
    nq = S // tq
    nh = H // hb
    qb, kb = _bias_columns(c2_pieces)
    vt = v.reshape(B, nq, tq, H, HEAD_DIM).transpose(0, 3, 1, 4, 2).reshape(B * H, nq, HEAD_DIM, tq)
    return pl.pallas_call(
        functools.partial(_fox_prompt_kernel, tq=tq, hb=hb, qsplit=qsplit),
        grid=(B, nh, nq),
        in_specs=[
            pl.BlockSpec((tq, hb * HEAD_DIM), lambda b, h, i: (b * nq + i, h)),
            pl.BlockSpec((hb, tq, LANES), lambda b, h, i: (b * nh + h, i, 0)),
            pl.BlockSpec((S, hb * HEAD_DIM), lambda b, h, i: (b, h)),
            pl.BlockSpec((hb, S, LANES), lambda b, h, i: (b * nh + h, 0, 0)),
            pl.BlockSpec((hb, nq, HEAD_DIM, tq), lambda b, h, i: (b * nh + h, 0, 0, 0)),
        ],
        out_specs=pl.BlockSpec((tq, hb * HEAD_DIM), lambda b, h, i: (b * nq + i, h)),
        out_shape=jax.ShapeDtypeStruct((B * S, WIDTH), BF16),
        compiler_params=_params(3),
        name="fox_prompt",
    )(q, qb, k, kb, vt)


def _causal_mask(tq, tk):
    rows = lax.broadcasted_iota(jnp.int32, (tq, tk), 0)
    cols = lax.broadcasted_iota(jnp.int32, (tq, tk), 1)
    return cols <= rows


def _softmax_update(s, m, l):
    m_new = jnp.maximum(m, jnp.max(s, axis=-1, keepdims=True))
    alpha = jnp.exp2(m - m_new)
    p = jnp.exp2(s - m_new)
    return m_new, alpha * l + jnp.sum(p, axis=-1, keepdims=True), alpha, p


def _fox_sample_kernel(q_ref, kc_ref, vc_ref, kn_ref, vn_ref, cq_ref, ckc_ref, ckn_ref, o_ref, m_scr, l_scr, acc_scr):
    j = pl.program_id(1)
    T = q_ref.shape[0]
    heads = [slice(h * HEAD_DIM, (h + 1) * HEAD_DIM) for h in range(N_HEADS)]

    @pl.when(j == 0)
    def _():
        m_scr[...] = jnp.full(m_scr.shape, NEG, F32)
        l_scr[...] = jnp.zeros(l_scr.shape, F32)
        acc_scr[...] = jnp.zeros(acc_scr.shape, F32)

    k_all = pltpu.einshape("mhd->hmd", kc_ref[...])
    v_all = pltpu.einshape("mhd->hmd", vc_ref[...])
    for h, hs in enumerate(heads):
        s = _dot_nt(q_ref[:, hs], k_all[h].astype(BF16)) + (cq_ref[h] - ckc_ref[h])
        m, l, alpha, p = _softmax_update(s, m_scr[h], l_scr[h])
        acc_scr[h] = alpha * acc_scr[h] + _dot(p.astype(BF16), v_all[h].astype(BF16))
        m_scr[h], l_scr[h] = m, l

    @pl.when(j == pl.num_programs(1) - 1)
    def _():
        mask = _causal_mask(T, T)
        for h, hs in enumerate(heads):
            s = jnp.where(mask, _dot_nt(q_ref[:, hs], kn_ref[:, hs]) + (cq_ref[h] - ckn_ref[h]), NEG)
            _, l, alpha, p = _softmax_update(s, m_scr[h], l_scr[h])
            acc = alpha * acc_scr[h] + _dot(p.astype(BF16), vn_ref[:, hs])
            o_ref[:, hs] = (acc / l).astype(o_ref.dtype)


def _fox_sample(q, k_new, v_new, cache_k, cache_v, c2, B, T, tk=1024):
    H = N_HEADS
    P = cache_k.shape[2]
    nk = P // tk
    c_q = c2[:, P:].reshape(B, H, T, 1)
    c_kc = c2[:, :P].reshape(B, H, nk, 1, tk)
    c_kn = c2[:, P:].reshape(B, H, 1, T)
    tok = pl.BlockSpec((T, WIDTH), lambda b, j: (b, 0))
    cache = pl.BlockSpec((None, None, tk, H, HEAD_DIM), lambda b, j: (0, b, j, 0, 0))
    return pl.pallas_call(
        _fox_sample_kernel,
        grid=(B, nk),
        in_specs=[
            tok, cache, cache, tok, tok,
            pl.BlockSpec((None, H, T, 1), lambda b, j: (b, 0, 0, 0)),
            pl.BlockSpec((None, H, None, 1, tk), lambda b, j: (b, 0, j, 0, 0)),
            pl.BlockSpec((None, H, 1, T), lambda b, j: (b, 0, 0, 0)),
        ],
        out_specs=tok,
        out_shape=jax.ShapeDtypeStruct((B * T, WIDTH), BF16),
        scratch_shapes=[pltpu.VMEM((H, T, 1), F32), pltpu.VMEM((H, T, 1), F32), pltpu.VMEM((H, T, HEAD_DIM), F32)],
        compiler_params=_params(2),
        name="fox_sample",
    )(q, cache_k, cache_v, k_new, v_new, c_q, c_kc, c_kn)


def _retention_kernel(q_ref, k_ref, v_ref, g_ref, gn_ref, s0_ref, dm_ref, qd_ref, kd_ref, cd_ref,
                      o_ref, s_ref, *, L, hb, unroll):
    T = q_ref.shape[0]

    def chunk(off, h, s):
        rows = pl.ds(off, L)
        hs = slice(h * HEAD_DIM, (h + 1) * HEAD_DIM)
        q = q_ref[rows, hs]
        k = k_ref[rows, hs]
        v = v_ref[rows, hs].astype(BF16)
        att = _dot_nt(q, k) * dm_ref[h]
        o = _dot(att.astype(BF16), v) + _dot((q.astype(F32) * qd_ref[h]).astype(BF16), s.astype(BF16))
        kd = k.astype(F32) * kd_ref[h]
        s_new = s * cd_ref[h] + _dot(kd.T.astype(BF16), v)
        mu = jnp.mean(o, axis=-1, keepdims=True)
        oc = o - mu
        y = oc * lax.rsqrt(jnp.mean(oc * oc, axis=-1, keepdims=True) + EPS)
        o_ref[rows, hs] = (y * gn_ref[:, hs] * _silu(g_ref[rows, hs])).astype(o_ref.dtype)
        return s_new

    def body(it, states):
        states = list(states)
        for u in range(unroll):
            off = pl.multiple_of((it * unroll + u) * L, L)
            for h in range(hb):
                states[h] = chunk(off, h, states[h])
        return tuple(states)

    states = lax.fori_loop(0, T // (L * unroll), body, tuple(s0_ref[h] for h in range(hb)))
    for h in range(hb):
        s_ref[h] = states[h]


def _retention_consts(L):
    H = N_HEADS
    log_g = jnp.log1p(-jnp.exp2(-5.0 - jnp.arange(H, dtype=F32)))
    i = jnp.arange(L, dtype=F32)
    diff = i[:, None] - i[None, :]
    dmat = jnp.where(diff[None] >= 0, jnp.exp(jnp.maximum(diff, 0.0)[None] * log_g[:, None, None]), 0.0)
    q_dec = jnp.exp((i + 1.0)[None, :] * log_g[:, None])
    k_dec = jnp.exp((L - 1.0 - i)[None, :] * log_g[:, None])
    c_dec = jnp.exp(L * log_g)
    rep = lambda a: jnp.broadcast_to(a[..., None], a.shape + (LANES,))
    return dmat, rep(q_dec), rep(k_dec), rep(c_dec[:, None])


def _retention(qk, vg, gn, s0, B, T):
    H = N_HEADS
    L = min(RET_CHUNK, T)
    n_chunks = T // L
    unroll = math.gcd(RET_UNROLL, n_chunks)
    hb = 1 if n_chunks > 1 else H
    nh = H // hb
    dmat, q_dec, k_dec, c_dec = _retention_consts(L)
    seq = lambda off: pl.BlockSpec((T, hb * HEAD_DIM), lambda b, h: (b, h + off))
    per_head = lambda r: pl.BlockSpec((hb, r, LANES), lambda b, h: (h, 0, 0))
    state = pl.BlockSpec((None, hb, HEAD_DIM, HEAD_DIM), lambda b, h: (b, h, 0, 0))
    return pl.pallas_call(
        functools.partial(_retention_kernel, L=L, hb=hb, unroll=unroll),
        grid=(B, nh),
        in_specs=[seq(0), seq(nh), seq(0), seq(nh),
                  pl.BlockSpec((1, hb * HEAD_DIM), lambda b, h: (0, h)), state,
                  pl.BlockSpec((hb, L, L), lambda b, h: (h, 0, 0)), per_head(L), per_head(L), per_head(1)],
        out_specs=[seq(0), state],
        out_shape=[jax.ShapeDtypeStruct((B * T, WIDTH), BF16), jax.ShapeDtypeStruct((B, H, HEAD_DIM, HEAD_DIM), F32)],
        compiler_params=_params(2),
        name="retention",
    )(qk, qk, vg, vg, gn.reshape(1, WIDTH), s0, dmat, q_dec, k_dec, c_dec)


def _conv_kernel(x_ref, prev_ref, init_ref, w_ref, b_ref, g_ref, beta_ref, o_ref, xp_ref, xs_ref, z_ref, *, tr):
    i = pl.program_id(1)

    @pl.when(i == 0)
    def _():
        xp_ref[0:HALO, :] = init_ref[...]

    @pl.when(i > 0)
    def _():
        xp_ref[0:HALO, :] = prev_ref[...]

    C = x_ref.shape[1]
    xp_ref[HALO:HALO + tr, :] = x_ref[...]
    xp_ref[HALO + tr:HALO + tr + SUBLANES, :] = jnp.zeros((SUBLANES, C), F32)
    first = HALO - (CONV_WIDTH - 1)
    span = tr + CONV_SPAN
    for r in range(SUBLANES):
        xs_ref[r] = xp_ref[pl.ds(first + r, span), :]

    groups = CONV_SPAN // SUBLANES + 1
    for c in range(C // LANES):
        cs = slice(c * LANES, (c + 1) * LANES)
        wts = [w_ref[w, :, cs] for w in range(CONV_WIDTH)]
        bias = jnp.broadcast_to(b_ref[:, cs], (SUBLANES, LANES))

        def step(u, accs, cs=cs, wts=wts, bias=bias):
            accs = (bias,) + accs
            r0 = pl.multiple_of(u * SUBLANES, SUBLANES)
            for r in range(SUBLANES):
                x = xs_ref[r, pl.ds(r0, SUBLANES), cs]
                accs = tuple(acc + x * wts[SUBLANES * a + r] if SUBLANES * a + r < CONV_WIDTH else acc
                             for a, acc in enumerate(accs))
            z_ref[pl.ds(r0, SUBLANES), cs] = accs[-1]
            return accs[:-1]

        lax.fori_loop(0, span // SUBLANES, step, (bias,) * (groups - 1), unroll=CONV_UNROLL)

    z = z_ref[CONV_SPAN:CONV_SPAN + tr, :]
    mu = jnp.mean(z, axis=-1, keepdims=True)
    zc = z - mu
    y = zc * lax.rsqrt(jnp.mean(zc * zc, axis=-1, keepdims=True) + EPS) * g_ref[...] + beta_ref[...]
    o_ref[...] = _silu(y).astype(o_ref.dtype)


def _conv_module(glu, init, w_dw, b_dw, ln_g, ln_b, B, T, tr):
    C = glu.shape[1]
    nt = T // tr
    per = tr // HALO
    w_rep = jnp.broadcast_to(jnp.pad(w_dw, ((0, HALO - CONV_WIDTH), (0, 0)))[:, None, :], (HALO, SUBLANES, C))
    vec = pl.BlockSpec((1, C), lambda b, i: (0, 0))
    return pl.pallas_call(
        functools.partial(_conv_kernel, tr=tr),
        grid=(B, nt),
        in_specs=[
            pl.BlockSpec((tr, C), lambda b, i: (b * nt + i, 0)),
            pl.BlockSpec((HALO, C), lambda b, i: (b * nt * per + jnp.maximum(i * per - 1, 0), 0)),
            pl.BlockSpec((None, HALO, C), lambda b, i: (b, 0, 0)),
            pl.BlockSpec((HALO, SUBLANES, C), lambda b, i: (0, 0, 0)),
            vec, vec, vec,
        ],
        out_specs=pl.BlockSpec((tr, C), lambda b, i: (b * nt + i, 0)),
        out_shape=jax.ShapeDtypeStruct((B * T, C), BF16),
        scratch_shapes=[pltpu.VMEM((HALO + tr + SUBLANES, C), F32),
                        pltpu.VMEM((SUBLANES, tr + CONV_SPAN, C), F32),
                        pltpu.VMEM((tr + CONV_SPAN, C), F32)],
        compiler_params=_params(2),
        name="conv_module",
    )(glu, glu, init, w_rep, b_dw.reshape(1, C), ln_g.reshape(1, C), ln_b.reshape(1, C))


def _rope_tables(pos):
    half = HEAD_DIM // 2
    inv = jnp.exp(-math.log(ROPE_BASE) * jnp.arange(half, dtype=F32) / half)
    ang = pos.astype(F32)[:, None] * inv[None, :]
    cos, sin = jnp.cos(ang), jnp.sin(ang)
    return jnp.concatenate([cos, cos], axis=-1), jnp.concatenate([-sin, sin], axis=-1)


def _prepare_weights(norm_mix, norm_ffn, w_in, b_forget, q_norm_gain, k_norm_gain, ret_norm_gain, w_out,
                     w_pw1, b_pw1, w_dw, b_dw, conv_ln_gain, conv_ln_bias, w_pw2, b_pw2,
                     w_ffn_gate, w_ffn_up, w_ffn_down):
    W = WIDTH
    D = w_in.shape[1]
    f0 = 3 * W
    f1 = f0 + N_HEADS
    w_b = w_in[0, :, f1:]
    return dict(
        norm_mix=norm_mix, norm_ffn=norm_ffn,
        w_q=_Rhs(w_in, 0, D, col=0), w_k=_Rhs(w_in, 0, D, col=W), w_v=_Rhs(w_in, 0, D, col=2 * W),
        w_f=_Rhs(w_in, 0, D, col=f0),
        b_f=jnp.pad(b_forget[0], (0, LANES - N_HEADS)).reshape(1, LANES),
        w_qkb=_Rhs(w_b, None, D, col=0), w_vgb=_Rhs(w_b, None, D, col=2 * W),
        rope_scale=jnp.concatenate([jnp.ones((1, W), F32), jnp.full((1, W), HEAD_DIM ** -0.5, F32)], axis=1),
        gq=q_norm_gain[0].reshape(1, HEAD_DIM), gk=k_norm_gain[0].reshape(1, HEAD_DIM), gn=ret_norm_gain[0],
        w_oa=_Rhs(w_out, 0, W, k_blk=0), w_ob=_Rhs(w_out, 0, W, k_blk=1),
        w_1a=_Rhs(w_pw1, 0, D, col=0), w_1g=_Rhs(w_pw1, 0, D, col=w_pw1.shape[2] // 2),
        b_1a=b_pw1[0, :2 * W].reshape(1, -1), b_1g=b_pw1[0, 2 * W:].reshape(1, -1),
        w_dw=w_dw[0], b_dw=b_dw[0], ln_g=conv_ln_gain[0], ln_b=conv_ln_bias[0],
        w_2=_Rhs(w_pw2, 0, w_pw2.shape[1]), b_2=b_pw2[0].reshape(1, -1),
        w_g=[_Rhs(w_ffn_gate, l, D) for l in range(2)], w_u=[_Rhs(w_ffn_up, l, D) for l in range(2)],
        w_d=[_Rhs(w_ffn_down, l, w_ffn_down.shape[1]) for l in range(2)],
        d_ff=w_ffn_gate.shape[2],
    )


def _ffn(x, P, layer, tag):
    D = x.shape[1]
    h = _rmsnorm(x, P["norm_ffn"][layer], f"rms_ffn{layer}_{tag}")
    act, = _matmul(f"ffn_up{layer}_{tag}", [h], [P["w_g"][layer], P["w_u"][layer]], P["d_ff"],
                   [[(0, 0)], [(0, 1)]], _ep_swiglu, [BF16], tm=1024)
    out, = _matmul(f"ffn_down{layer}_{tag}", [act], [P["w_d"][layer]], D, [[(0, 0)]], _ep_residual, [F32],
                   tile=[x])
    return out


def _trunk(x3, pos, past, conv_past, P, tag):
    B, T, D = x3.shape
    M = B * T
    H = N_HEADS
    W = WIDTH
    x = x3.reshape(M, D)
    one = [[(0, 0)]]
    tall = dict(tm=1024)
    wide = dict(tm=1024, tn=1024)

    h = _rmsnorm(x, P["norm_mix"][0], f"rms_mix0_{tag}")
    q, = _matmul(f"proj_q_{tag}", [h], [P["w_q"]], W, one,
                 functools.partial(_ep_headnorm, scales=(HEAD_DIM ** -0.5 * LOG2E,)), [BF16], full=[P["gq"]], **wide)
    k32, k16 = _matmul(f"proj_k_{tag}", [h], [P["w_k"]], W, one,
                       functools.partial(_ep_headnorm, scales=(1.0, 1.0)), [F32, BF16], full=[P["gk"]], **wide)
    v32, v16 = _matmul(f"proj_v_{tag}", [h], [P["w_v"]], W, one, _ep_identity, [F32, BF16], **wide)
    logf_pad, = _matmul(f"proj_f_{tag}", [h], [P["w_f"]], LANES, one, _ep_logsigmoid, [F32], col=[P["b_f"]],
                        tn=LANES)
    cos, sin = _rope_tables(pos)
    reps = max(1, wide["tm"] // T)
    cos, sin = jnp.tile(cos, (reps, 1)), jnp.tile(sin, (reps, 1))
    qk_b, = _matmul(f"proj_qkb_{tag}", [h], [P["w_qkb"]], 2 * W, one, _ep_rope, [BF16], col=[P["rope_scale"]],
                    row=[cos, sin], **wide)
    vg_b, = _matmul(f"proj_vgb_{tag}", [h], [P["w_vgb"]], 2 * W, one, _ep_identity, [F32], **wide)

    logf = logf_pad[:, :H].reshape(B, T, H)
    lf_t = logf.transpose(0, 2, 1).reshape(B * H, T)
    if past is None:
        _, c2_pieces = _cumsum_rows(lf_t, f"cumsum_{tag}", LOG2E)
        oa = _fox_prompt(q, k16, v16, c2_pieces, B, T)
        s0 = jnp.zeros((B, H, HEAD_DIM, HEAD_DIM), F32)
    else:
        kc, vc, lfc, s0 = past
        Pl = kc.shape[2]
        lf_all = jnp.concatenate([lfc.transpose(0, 2, 1).reshape(B * H, Pl), lf_t], axis=1)
        pad = (-lf_all.shape[1]) % LANES
        c2, _ = _cumsum_rows(jnp.pad(lf_all, ((0, 0), (0, pad))), f"cumsum_{tag}", LOG2E)
        oa = _fox_sample(q, k16, v16, kc, vc, c2[:, :Pl + T], B, T)
    ob, s_new = _retention(qk_b, vg_b, P["gn"], s0, B, T)
    x, = _matmul(f"out_proj_{tag}", [oa, ob], [P["w_oa"], P["w_ob"]], D, [[(0, 0), (1, 1)]], _ep_residual, [F32],
                 tile=[x], **wide)
    x = _ffn(x, P, 0, tag)

    h = _rmsnorm(x, P["norm_mix"][1], f"rms_mix1_{tag}")
    glu, = _matmul(f"conv_pw1_{tag}", [h], [P["w_1a"], P["w_1g"]], D, [[(0, 0)], [(0, 1)]], _ep_glu, [F32],
                   col=[P["b_1a"], P["b_1g"]], **tall)
    keep = CONV_WIDTH - 1
    assert T >= keep
    if conv_past is None:
        init = jnp.zeros((B, HALO, D), F32)
    else:
        init = jnp.pad(conv_past, ((0, 0), (HALO - keep, 0), (0, 0)))
    z = _conv_module(glu, init, P["w_dw"], P["b_dw"], P["ln_g"], P["ln_b"], B, T, tr=min(256, T))
    conv_new = glu.reshape(B, T, D)[:, -keep:]
    x, = _matmul(f"conv_pw2_{tag}", [z], [P["w_2"]], D, one, _ep_bias_residual, [F32], tile=[x], col=[P["b_2"]],
                 **wide)
    x = _ffn(x, P, 1, tag)

    return (x.reshape(B, T, D), k32.reshape(1, B, T, H, HEAD_DIM), v32.reshape(1, B, T, H, HEAD_DIM),
            logf[None], s_new[None], conv_new[None])


def kernel(x_prompt, x_sample, cache_k, cache_v, cache_logf, state_ret, state_conv,
           norm_mix, norm_ffn, w_in, b_forget, q_norm_gain, k_norm_gain, ret_norm_gain, w_out,
           w_pw1, b_pw1, w_dw, b_dw, conv_ln_gain, conv_ln_bias, w_pw2, b_pw2,
           w_ffn_gate, w_ffn_up, w_ffn_down):
    P = _prepare_weights(norm_mix, norm_ffn, w_in, b_forget, q_norm_gain, k_norm_gain, ret_norm_gain, w_out,
                         w_pw1, b_pw1, w_dw, b_dw, conv_ln_gain, conv_ln_bias, w_pw2, b_pw2,
                         w_ffn_gate, w_ffn_up, w_ffn_down)
    pos_p = jnp.arange(x_prompt.shape[1])
    y_p, k_p, v_p, lf_p, r_p, c_p = _trunk(x_prompt, pos_p, None, None, P, "p")
    _, Bs, Pl, H, hd = cache_k.shape
    pos_s = Pl + jnp.arange(x_sample.shape[1])
    past = (cache_k, cache_v, cache_logf.reshape(Bs, Pl, H), state_ret.reshape(Bs, H, hd, hd))
    conv_past = state_conv.reshape(state_conv.shape[1:])
    y_s, k_s, v_s, lf_s, r_s, c_s = _trunk(x_sample, pos_s, past, conv_past, P, "s")
    return (y_p, y_s, k_p, v_p, lf_p, r_p, c_p, k_s, v_s, lf_s, r_s, c_s)
```

```python
import functools
import math
from typing import NamedTuple, Optional

import jax
import jax.numpy as jnp
from jax import lax
from jax.experimental import pallas as pl
from jax.experimental.pallas import tpu as pltpu

F32 = jnp.float32
BF16 = jnp.bfloat16

LANES = 128
SUBLANES = 8
HALO = 32
VMEM_LIMIT = 56 * 1024 * 1024

HEAD_DIM = 128
N_HEADS = 8
WIDTH = N_HEADS * HEAD_DIM
CONV_WIDTH = 31
ROPE_BASE = 10000.0
EPS = 1e-6
NEG = -1e30
LOG2E = math.log2(math.e)
CONV_SPAN = (CONV_WIDTH - 1) // SUBLANES * SUBLANES
CONV_UNROLL = 5
RET_CHUNK = 256
RET_UNROLL = 4


def _params(n_grid):
    return pltpu.CompilerParams(dimension_semantics=("arbitrary",) * n_grid, vmem_limit_bytes=VMEM_LIMIT)


def _silu(x):
    return x * jax.nn.sigmoid(x)


def _dot(a, b):
    return jnp.dot(a, b, preferred_element_type=F32)


def _dot_nt(a, b):
    return lax.dot_general(a, b, (((1,), (1,)), ((), ())), preferred_element_type=F32)


def _rms_kernel(x_ref, g_ref, o_ref):
    x = x_ref[...]
    ms = jnp.mean(x * x, axis=-1, keepdims=True)
    o_ref[...] = (x * lax.rsqrt(ms + EPS) * g_ref[...]).astype(o_ref.dtype)


def _rmsnorm(x, g, name, tm=512):
    M, D = x.shape
    return pl.pallas_call(
        _rms_kernel,
        grid=(M // tm,),
        in_specs=[pl.BlockSpec((tm, D), lambda i: (i, 0)), pl.BlockSpec((1, D), lambda i: (0, 0))],
        out_specs=pl.BlockSpec((tm, D), lambda i: (i, 0)),
        out_shape=jax.ShapeDtypeStruct((M, D), BF16),
        compiler_params=_params(1),
        name=name,
    )(x, g.reshape(1, D))


class _Rhs(NamedTuple):
    arr: jax.Array
    lead: Optional[int]
    k: int
    k_blk: int = 0
    col: int = 0


def _mm_kernel(*refs, n_lhs, n_rhs, n_extra, n_out, products, epilogue):
    lhs = refs[:n_lhs]
    rhs = refs[n_lhs:n_lhs + n_rhs]
    extras = refs[n_lhs + n_rhs:n_lhs + n_rhs + n_extra]
    outs = refs[n_lhs + n_rhs + n_extra:n_lhs + n_rhs + n_extra + n_out]
    wbuf = refs[n_lhs + n_rhs + n_extra + n_out:]

    @pl.when(pl.program_id(1) == 0)
    def _():
        for w, b in zip(rhs, wbuf):
            b[...] = w[...].astype(BF16)

    accs = []
    for prod in products:
        acc = None
        for a, b in prod:
            d = _dot(lhs[a][...], wbuf[b][...])
            acc = d if acc is None else acc + d
        accs.append(acc)
    epilogue(accs, extras, outs)


def _matmul(name, lhs, rhs, n_cols, products, epilogue, out_dtypes, *, tile=(), col=(), row=(), full=(),
            tm=512, tn=512):
    M = lhs[0].shape[0]
    assert M % tm == 0 and n_cols % tn == 0
    in_specs = [pl.BlockSpec((tm, a.shape[1]), lambda j, i: (i, 0)) for a in lhs]
    for w in rhs:
        assert w.col % tn == 0
        c0 = w.col // tn
        if w.lead is None:
            in_specs.append(pl.BlockSpec((w.k, tn), lambda j, i, kb=w.k_blk, c0=c0: (kb, j + c0)))
        else:
            in_specs.append(pl.BlockSpec((None, w.k, tn),
                                         lambda j, i, ld=w.lead, kb=w.k_blk, c0=c0: (ld, kb, j + c0)))
    in_specs += [pl.BlockSpec((tm, tn), lambda j, i: (i, j)) for _ in tile]
    in_specs += [pl.BlockSpec((1, tn), lambda j, i: (0, j)) for _ in col]
    for r in row:
        assert r.shape[0] % tm == 0
        in_specs.append(pl.BlockSpec((tm, r.shape[1]), lambda j, i, nblk=r.shape[0] // tm: (i % nblk, 0)))
    in_specs += [pl.BlockSpec(f.shape, lambda j, i, nd=f.ndim: (0,) * nd) for f in full]
    extras = tuple(tile) + tuple(col) + tuple(row) + tuple(full)
    kern = functools.partial(_mm_kernel, n_lhs=len(lhs), n_rhs=len(rhs), n_extra=len(extras),
                             n_out=len(out_dtypes), products=products, epilogue=epilogue)
    return pl.pallas_call(
        kern,
        grid=(n_cols // tn, M // tm),
        in_specs=in_specs,
        out_specs=[pl.BlockSpec((tm, tn), lambda j, i: (i, j)) for _ in out_dtypes],
        out_shape=[jax.ShapeDtypeStruct((M, n_cols), dt) for dt in out_dtypes],
        scratch_shapes=[pltpu.VMEM((w.k, tn), BF16) for w in rhs],
        compiler_params=_params(2),
        name=name,
    )(*lhs, *[w.arr for w in rhs], *extras)


def _ep_headnorm(accs, extras, outs, *, scales):
    gain = extras[0][...]
    y = accs[0]
    for h in range(y.shape[1] // HEAD_DIM):
        sl = slice(h * HEAD_DIM, (h + 1) * HEAD_DIM)
        yh = y[:, sl]
        ms = jnp.mean(yh * yh, axis=-1, keepdims=True)
        r = yh * lax.rsqrt(ms + EPS) * gain
        for o, sc in zip(outs, scales):
            o[:, sl] = (r if sc == 1.0 else r * sc).astype(o.dtype)


def _ep_identity(accs, extras, outs):
    for o in outs:
        o[...] = accs[0].astype(o.dtype)


def _ep_logsigmoid(accs, extras, outs):
    z = accs[0] + extras[0][...]
    outs[0][...] = -(jnp.maximum(-z, 0.0) + jnp.log1p(jnp.exp(-jnp.abs(z))))


def _ep_rope(accs, extras, outs):
    scale, cos, sin = extras[0][...], extras[1][...], extras[2][...]
    y = accs[0]
    for h in range(y.shape[1] // HEAD_DIM):
        sl = slice(h * HEAD_DIM, (h + 1) * HEAD_DIM)
        yh = y[:, sl]
        r = yh * cos + pltpu.roll(yh, HEAD_DIM // 2, 1) * sin
        outs[0][:, sl] = (r * scale[:, sl]).astype(outs[0].dtype)


def _ep_residual(accs, extras, outs):
    outs[0][...] = extras[0][...] + accs[0]


def _ep_bias_residual(accs, extras, outs):
    outs[0][...] = extras[0][...] + (accs[0] + extras[1][...])


def _ep_swiglu(accs, extras, outs):
    outs[0][...] = (_silu(accs[0]) * accs[1]).astype(outs[0].dtype)


def _ep_glu(accs, extras, outs):
    a = accs[0] + extras[0][...]
    g = accs[1] + extras[1][...]
    outs[0][...] = a * jax.nn.sigmoid(g)


def _split3(x):
    hi = x.astype(BF16)
    r1 = x - hi.astype(F32)
    mid = r1.astype(BF16)
    lo = (r1 - mid.astype(F32)).astype(BF16)
    return hi, mid, lo


def _cumsum_kernel(x_ref, c_ref, hi_ref, mid_ref, lo_ref, *, scale):
    nblk, R, _ = x_ref.shape
    row = lax.broadcasted_iota(jnp.int32, (LANES, LANES), 0)
    col = lax.broadcasted_iota(jnp.int32, (LANES, LANES), 1)
    tri = jnp.where(row <= col, 1.0, 0.0).astype(BF16)
    ones = jnp.ones((LANES, LANES), BF16)

    def body(b, carry):
        hi, mid, lo = _split3(x_ref[b])
        within = _dot(hi, tri) + _dot(mid, tri) + _dot(lo, tri)
        total = _dot(hi, ones) + _dot(mid, ones) + _dot(lo, ones)
        c = (carry + within) * scale
        c_ref[b] = c
        hi_ref[b], mid_ref[b], lo_ref[b] = _split3(c)
        return carry + total

    lax.fori_loop(0, nblk, body, jnp.zeros((R, LANES), F32))


def _cumsum_rows(x, name, scale):
    R, N = x.shape
    nblk = N // LANES
    x3 = x.reshape(R, nblk, LANES).transpose(1, 0, 2)
    outs = pl.pallas_call(
        functools.partial(_cumsum_kernel, scale=scale),
        out_shape=[jax.ShapeDtypeStruct((nblk, R, LANES), dt) for dt in (F32, BF16, BF16, BF16)],
        compiler_params=pltpu.CompilerParams(vmem_limit_bytes=VMEM_LIMIT),
        name=name,
    )(x3)
    c, hi, mid, lo = [o.transpose(1, 0, 2).reshape(R, N) for o in outs]
    return c, (hi, mid, lo)


def _flash_step_t(st, vt, carry, mask):
    m, l, acc = carry
    if mask is not None:
        st = jnp.where(mask, st, NEG)
    m_new = jnp.maximum(m, jnp.max(st, axis=0, keepdims=True))
    alpha = jnp.exp2(m - m_new)
    pt = jnp.exp2(st - m_new)
    l = alpha * l + jnp.sum(pt, axis=0, keepdims=True)
    acc = alpha * acc + _dot(vt, pt.astype(BF16))
    return m_new, l, acc


def _flash_init_t(tq):
    return (jnp.full((1, tq), NEG, F32), jnp.zeros((1, tq), F32), jnp.zeros((HEAD_DIM, tq), F32))


def _causal_mask_t(tk, tq, q_off):
    keys = lax.broadcasted_iota(jnp.int32, (tk, tq), 0)
    queries = lax.broadcasted_iota(jnp.int32, (tk, tq), 1) + q_off
    return keys <= queries


def _fox_prompt_kernel(q_ref, qb_ref, k_ref, kb_ref, vt_ref, o_ref, *, tq, hb, qsplit):
    i = pl.program_id(2)
    tqc = tq // qsplit
    heads = [slice(h * HEAD_DIM, (h + 1) * HEAD_DIM) for h in range(hb)]
    chains = [(h, slice(c * tqc, (c + 1) * tqc)) for h in range(hb) for c in range(qsplit)]
    qs = [jnp.concatenate([q_ref[qr, heads[h]], qb_ref[h, qr, :]], axis=1) for h, qr in chains]

    def block(j, carries, diagonal):
        rows = pl.ds(pl.multiple_of(j * tq, tq), tq)
        ks = [jnp.concatenate([k_ref[rows, hs], kb_ref[h, rows, :]], axis=1) for h, hs in enumerate(heads)]
        sts = [_dot_nt(ks[h], qs[n]) for n, (h, _) in enumerate(chains)]
        return tuple(
            _flash_step_t(sts[n], vt_ref[h, j], carries[n], _causal_mask_t(tq, tqc, qr.start) if diagonal else None)
            for n, (h, qr) in enumerate(chains))

    carries = lax.fori_loop(0, i, lambda j, c: block(j, c, False), (_flash_init_t(tqc),) * len(chains))
    carries = block(i, carries, True)
    for (m, l, acc), (h, qr) in zip(carries, chains):
        o_ref[qr, heads[h]] = (acc / l).T.astype(o_ref.dtype)


def _bias_columns(pieces):
    parts = jnp.stack(pieces, axis=-1)
    ones = jnp.ones_like(parts)
    pad = ((0, 0), (0, 0), (0, LANES - 6))
    return (jnp.pad(jnp.concatenate([parts, ones], axis=-1), pad),
            jnp.pad(jnp.concatenate([ones, -parts], axis=-1), pad))


def _fox_prompt(q, k, v, c2_pieces, B, S, tq=512, hb=2, qsplit=2):
    H = N_HEADS
    nq = S // tq
    nh = H // hb
    qb, kb = _bias_columns(c2_pieces)
    vt = v.reshape(B, nq, tq, H, HEAD_DIM).transpose(0, 3, 1, 4, 2).reshape(B * H, nq, HEAD_DIM, tq)
    return pl.pallas_call(
        functools.partial(_fox_prompt_kernel, tq=tq, hb=hb, qsplit=qsplit),
        grid=(B, nh, nq),
        in_specs=[
            pl.BlockSpec((tq, hb * HEAD_DIM), lambda b, h, i: (b * nq + i, h)),
            pl.BlockSpec((hb, tq, LANES), lambda b, h, i: (b * nh + h, i, 0)),
            pl.BlockSpec((S, hb * HEAD_DIM), lambda b, h, i: (b, h)),
            pl.BlockSpec((hb, S, LANES), lambda b, h, i: (b * nh + h, 0, 0)),
            pl.BlockSpec((hb, nq, HEAD_DIM, tq), lambda b, h, i: (b * nh + h, 0, 0, 0)),
        ],
        out_specs=pl.BlockSpec((tq, hb * HEAD_DIM), lambda b, h, i: (b * nq + i, h)),
        out_shape=jax.ShapeDtypeStruct((B * S, WIDTH), BF16),
        compiler_params=_params(3),
        name="fox_prompt",
    )(q, qb, k, kb, vt)


def _causal_mask(tq, tk):
    rows = lax.broadcasted_iota(jnp.int32, (tq, tk), 0)
    cols = lax.broadcasted_iota(jnp.int32, (tq, tk), 1)
    return cols <= rows


def _softmax_update(s, m, l):
    m_new = jnp.maximum(m, jnp.max(s, axis=-1, keepdims=True))
    alpha = jnp.exp2(m - m_new)
    p = jnp.exp2(s - m_new)
    return m_new, alpha * l + jnp.sum(p, axis=-1, keepdims=True), alpha, p


def _fox_sample_kernel(q_ref, kc_ref, vc_ref, kn_ref, vn_ref, cq_ref, ckc_ref, ckn_ref, o_ref, m_scr, l_scr, acc_scr):
    j = pl.program_id(1)
    T = q_ref.shape[0]
    heads = [slice(h * HEAD_DIM, (h + 1) * HEAD_DIM) for h in range(N_HEADS)]

    @pl.when(j == 0)
    def _():
        m_scr[...] = jnp.full(m_scr.shape, NEG, F32)
        l_scr[...] = jnp.zeros(l_scr.shape, F32)
        acc_scr[...] = jnp.zeros(acc_scr.shape, F32)

    k_all = pltpu.einshape("mhd->hmd", kc_ref[...])
    v_all = pltpu.einshape("mhd->hmd", vc_ref[...])
    for h, hs in enumerate(heads):
        s = _dot_nt(q_ref[:, hs], k_all[h].astype(BF16)) + (cq_ref[h] - ckc_ref[h])
        m, l, alpha, p = _softmax_update(s, m_scr[h], l_scr[h])
        acc_scr[h] = alpha * acc_scr[h] + _dot(p.astype(BF16), v_all[h].astype(BF16))
        m_scr[h], l_scr[h] = m, l

    @pl.when(j == pl.num_programs(1) - 1)
    def _():
        mask = _causal_mask(T, T)
        for h, hs in enumerate(heads):
            s = jnp.where(mask, _dot_nt(q_ref[:, hs], kn_ref[:, hs]) + (cq_ref[h] - ckn_ref[h]), NEG)
            _, l, alpha, p = _softmax_update(s, m_scr[h], l_scr[h])
            acc = alpha * acc_scr[h] + _dot(p.astype(BF16), vn_ref[:, hs])
            o_ref[:, hs] = (acc / l).astype(o_ref.dtype)


def _fox_sample(q, k_new, v_new, cache_k, cache_v, c2, B, T, tk=1024):
    H = N_HEADS
    P = cache_k.shape[2]
    nk = P // tk
    c_q = c2[:, P:].reshape(B, H, T, 1)
    c_kc = c2[:, :P].reshape(B, H, nk, 1, tk)
    c_kn = c2[:, P:].reshape(B, H, 1, T)
    tok = pl.BlockSpec((T, WIDTH), lambda b, j: (b, 0))
    cache = pl.BlockSpec((None, None, tk, H, HEAD_DIM), lambda b, j: (0, b, j, 0, 0))
    return pl.pallas_call(
        _fox_sample_kernel,
        grid=(B, nk),
        in_specs=[
            tok, cache, cache, tok, tok,
            pl.BlockSpec((None, H, T, 1), lambda b, j: (b, 0, 0, 0)),
            pl.BlockSpec((None, H, None, 1, tk), lambda b, j: (b, 0, j, 0, 0)),
            pl.BlockSpec((None, H, 1, T), lambda b, j: (b, 0, 0, 0)),
        ],
        out_specs=tok,
        out_shape=jax.ShapeDtypeStruct((B * T, WIDTH), BF16),
        scratch_shapes=[pltpu.VMEM((H, T, 1), F32), pltpu.VMEM((H, T, 1), F32), pltpu.VMEM((H, T, HEAD_DIM), F32)],
        compiler_params=_params(2),
        name="fox_sample",
    )(q, cache_k, cache_v, k_new, v_new, c_q, c_kc, c_kn)


def _retention_kernel(q_ref, k_ref, v_ref, g_ref, gn_ref, s0_ref, dm_ref, qd_ref, kd_ref, cd_ref,
                      o_ref, s_ref, *, L, hb, unroll):
    T = q_ref.shape[0]

    def chunk(off, h, s):
        rows = pl.ds(off, L)
        hs = slice(h * HEAD_DIM, (h + 1) * HEAD_DIM)
        q = q_ref[rows, hs]
        k = k_ref[rows, hs]
        v = v_ref[rows, hs].astype(BF16)
        att = _dot_nt(q, k) * dm_ref[h]
        o = _dot(att.astype(BF16), v) + _dot((q.astype(F32) * qd_ref[h]).astype(BF16), s.astype(BF16))
        kd = k.astype(F32) * kd_ref[h]
        s_new = s * cd_ref[h] + _dot(kd.T.astype(BF16), v)
        mu = jnp.mean(o, axis=-1, keepdims=True)
        oc = o - mu
        y = oc * lax.rsqrt(jnp.mean(oc * oc, axis=-1, keepdims=True) + EPS)
        o_ref[rows, hs] = (y * gn_ref[:, hs] * _silu(g_ref[rows, hs])).astype(o_ref.dtype)
        return s_new

    def body(it, states):
        states = list(states)
        for u in range(unroll):
            off = pl.multiple_of((it * unroll + u) * L, L)
            for h in range(hb):
                states[h] = chunk(off, h, states[h])
        return tuple(states)

    states = lax.fori_loop(0, T // (L * unroll), body, tuple(s0_ref[h] for h in range(hb)))
    for h in range(hb):
        s_ref[h] = states[h]


def _retention_consts(L):
    H = N_HEADS
    log_g = jnp.log1p(-jnp.exp2(-5.0 - jnp.arange(H, dtype=F32)))
    i = jnp.arange(L, dtype=F32)
    diff = i[:, None] - i[None, :]
    dmat = jnp.where(diff[None] >= 0, jnp.exp(jnp.maximum(diff, 0.0)[None] * log_g[:, None, None]), 0.0)
    q_dec = jnp.exp((i + 1.0)[None, :] * log_g[:, None])
    k_dec = jnp.exp((L - 1.0 - i)[None, :] * log_g[:, None])
    c_dec = jnp.exp(L * log_g)
    rep = lambda a: jnp.broadcast_to(a[..., None], a.shape + (LANES,))
    return dmat, rep(q_dec), rep(k_dec), rep(c_dec[:, None])


def _retention(qk, vg, gn, s0, B, T):
    H = N_HEADS
    L = min(RET_CHUNK, T)
    n_chunks = T // L
    unroll = math.gcd(RET_UNROLL, n_chunks)
    hb = 1 if n_chunks > 1 else H
    nh = H // hb
    dmat, q_dec, k_dec, c_dec = _retention_consts(L)
    seq = lambda off: pl.BlockSpec((T, hb * HEAD_DIM), lambda b, h: (b, h + off))
    per_head = lambda r: pl.BlockSpec((hb, r, LANES), lambda b, h: (h, 0, 0))
    state = pl.BlockSpec((None, hb, HEAD_DIM, HEAD_DIM), lambda b, h: (b, h, 0, 0))
    return pl.pallas_call(
        functools.partial(_retention_kernel, L=L, hb=hb, unroll=unroll),
        grid=(B, nh),
        in_specs=[seq(0), seq(nh), seq(0), seq(nh),
                  pl.BlockSpec((1, hb * HEAD_DIM), lambda b, h: (0, h)), state,
                  pl.BlockSpec((hb, L, L), lambda b, h: (h, 0, 0)), per_head(L), per_head(L), per_head(1)],
        out_specs=[seq(0), state],
        out_shape=[jax.ShapeDtypeStruct((B * T, WIDTH), BF16), jax.ShapeDtypeStruct((B, H, HEAD_DIM, HEAD_DIM), F32)],
        compiler_params=_params(2),
        name="retention",
    )(qk, qk, vg, vg, gn.reshape(1, WIDTH), s0, dmat, q_dec, k_dec, c_dec)


def _conv_kernel(x_ref, prev_ref, init_ref, w_ref, b_ref, g_ref, beta_ref, o_ref, xp_ref, xs_ref, z_ref, *, tr):
    i = pl.program_id(1)

    @pl.when(i == 0)
    def _():
        xp_ref[0:HALO, :] = init_ref[...]

    @pl.when(i > 0)
    def _():
        xp_ref[0:HALO, :] = prev_ref[...]

    C = x_ref.shape[1]
    xp_ref[HALO:HALO + tr, :] = x_ref[...]
    xp_ref[HALO + tr:HALO + tr + SUBLANES, :] = jnp.zeros((SUBLANES, C), F32)
    first = HALO - (CONV_WIDTH - 1)
    span = tr + CONV_SPAN
    for r in range(SUBLANES):
        xs_ref[r] = xp_ref[pl.ds(first + r, span), :]

    groups = CONV_SPAN // SUBLANES + 1
    for c in range(C // LANES):
        cs = slice(c * LANES, (c + 1) * LANES)
        wts = [w_ref[w, :, cs] for w in range(CONV_WIDTH)]
        bias = jnp.broadcast_to(b_ref[:, cs], (SUBLANES, LANES))

        def step(u, accs, cs=cs, wts=wts, bias=bias):
            accs = (bias,) + accs
            r0 = pl.multiple_of(u * SUBLANES, SUBLANES)
            for r in range(SUBLANES):
                x = xs_ref[r, pl.ds(r0, SUBLANES), cs]
                accs = tuple(acc + x * wts[SUBLANES * a + r] if SUBLANES * a + r < CONV_WIDTH else acc
                             for a, acc in enumerate(accs))
            z_ref[pl.ds(r0, SUBLANES), cs] = accs[-1]
            return accs[:-1]

        lax.fori_loop(0, span // SUBLANES, step, (bias,) * (groups - 1), unroll=CONV_UNROLL)

    z = z_ref[CONV_SPAN:CONV_SPAN + tr, :]
    mu = jnp.mean(z, axis=-1, keepdims=True)
    zc = z - mu
    y = zc * lax.rsqrt(jnp.mean(zc * zc, axis=-1, keepdims=True) + EPS) * g_ref[...] + beta_ref[...]
    o_ref[...] = _silu(y).astype(o_ref.dtype)


def _conv_module(glu, init, w_dw, b_dw, ln_g, ln_b, B, T, tr):
    C = glu.shape[1]
    nt = T // tr
    per = tr // HALO
    w_rep = jnp.broadcast_to(jnp.pad(w_dw, ((0, HALO - CONV_WIDTH), (0, 0)))[:, None, :], (HALO, SUBLANES, C))
    vec = pl.BlockSpec((1, C), lambda b, i: (0, 0))
    return pl.pallas_call(
        functools.partial(_conv_kernel, tr=tr),
        grid=(B, nt),
        in_specs=[
            pl.BlockSpec((tr, C), lambda b, i: (b * nt + i, 0)),
            pl.BlockSpec((HALO, C), lambda b, i: (b * nt * per + jnp.maximum(i * per - 1, 0), 0)),
            pl.BlockSpec((None, HALO, C), lambda b, i: (b, 0, 0)),
            pl.BlockSpec((HALO, SUBLANES, C), lambda b, i: (0, 0, 0)),
            vec, vec, vec,
        ],
        out_specs=pl.BlockSpec((tr, C), lambda b, i: (b * nt + i, 0)),
        out_shape=jax.ShapeDtypeStruct((B * T, C), BF16),
        scratch_shapes=[pltpu.VMEM((HALO + tr + SUBLANES, C), F32),
                        pltpu.VMEM((SUBLANES, tr + CONV_SPAN, C), F32),
                        pltpu.VMEM((tr + CONV_SPAN, C), F32)],
        compiler_params=_params(2),
        name="conv_module",
    )(glu, glu, init, w_rep, b_dw.reshape(1, C), ln_g.reshape(1, C), ln_b.reshape(1, C))


def _rope_tables(pos):
    half = HEAD_DIM // 2
    inv = jnp.exp(-math.log(ROPE_BASE) * jnp.arange(half, dtype=F32) / half)
    ang = pos.astype(F32)[:, None] * inv[None, :]
    cos, sin = jnp.cos(ang), jnp.sin(ang)
    return jnp.concatenate([cos, cos], axis=-1), jnp.concatenate([-sin, sin], axis=-1)


def _prepare_weights(norm_mix, norm_ffn, w_in, b_forget, q_norm_gain, k_norm_gain, ret_norm_gain, w_out,
                     w_pw1, b_pw1, w_dw, b_dw, conv_ln_gain, conv_ln_bias, w_pw2, b_pw2,
                     w_ffn_gate, w_ffn_up, w_ffn_down):
    W = WIDTH
    D = w_in.shape[1]
    f0 = 3 * W
    f1 = f0 + N_HEADS
    w_b = w_in[0, :, f1:]
    return dict(
        norm_mix=norm_mix, norm_ffn=norm_ffn,
        w_q=_Rhs(w_in, 0, D, col=0), w_k=_Rhs(w_in, 0, D, col=W), w_v=_Rhs(w_in, 0, D, col=2 * W),
        w_f=_Rhs(w_in, 0, D, col=f0),
        b_f=jnp.pad(b_forget[0], (0, LANES - N_HEADS)).reshape(1, LANES),
        w_qkb=_Rhs(w_b, None, D, col=0), w_vgb=_Rhs(w_b, None, D, col=2 * W),
        rope_scale=jnp.concatenate([jnp.ones((1, W), F32), jnp.full((1, W), HEAD_DIM ** -0.5, F32)], axis=1),
        gq=q_norm_gain[0].reshape(1, HEAD_DIM), gk=k_norm_gain[0].reshape(1, HEAD_DIM), gn=ret_norm_gain[0],
        w_oa=_Rhs(w_out, 0, W, k_blk=0), w_ob=_Rhs(w_out, 0, W, k_blk=1),
        w_1a=_Rhs(w_pw1, 0, D, col=0), w_1g=_Rhs(w_pw1, 0, D, col=w_pw1.shape[2] // 2),
        b_1a=b_pw1[0, :2 * W].reshape(1, -1), b_1g=b_pw1[0, 2 * W:].reshape(1, -1),
        w_dw=w_dw[0], b_dw=b_dw[0], ln_g=conv_ln_gain[0], ln_b=conv_ln_bias[0],
        w_2=_Rhs(w_pw2, 0, w_pw2.shape[1]), b_2=b_pw2[0].reshape(1, -1),
        w_g=[_Rhs(w_ffn_gate, l, D) for l in range(2)], w_u=[_Rhs(w_ffn_up, l, D) for l in range(2)],
        w_d=w_ffn_down,
        d_ff=w_ffn_gate.shape[2],
    )


def _deep_residual_kernel(a_ref, w_ref, x_ref, o_ref, wbuf, acc, *, nk):
    i = pl.program_id(1)
    k = pl.program_id(2)

    @pl.when(i == 0)
    def _():
        wbuf[k] = w_ref[...].astype(BF16)

    d = _dot(a_ref[...], wbuf[k])

    @pl.when(k == 0)
    def _():
        acc[...] = d

    @pl.when(jnp.logical_and(k > 0, k < nk - 1))
    def _():
        acc[...] += d

    @pl.when(k == nk - 1)
    def _():
        o_ref[...] = x_ref[...] + (acc[...] + d)


def _deep_residual_matmul(name, a, w, layer, x, *, tm=1024, tn=512, nk=2):
    M, K = a.shape
    N = x.shape[1]
    tk = K // nk
    assert M % tm == 0 and N % tn == 0 and K % nk == 0 and tk % LANES == 0
    return pl.pallas_call(
        functools.partial(_deep_residual_kernel, nk=nk),
        grid=(N // tn, M // tm, nk),
        in_specs=[
            pl.BlockSpec((tm, tk), lambda j, i, k: (i, k)),
            pl.BlockSpec((None, tk, tn), lambda j, i, k: (layer, jnp.where(i == 0, k, nk - 1), j)),
            pl.BlockSpec((tm, tn), lambda j, i, k: (i, j)),
        ],
        out_specs=pl.BlockSpec((tm, tn), lambda j, i, k: (i, j)),
        out_shape=jax.ShapeDtypeStruct((M, N), F32),
        scratch_shapes=[pltpu.VMEM((nk, tk, tn), BF16), pltpu.VMEM((tm, tn), F32)],
        compiler_params=_params(3),
        name=name,
    )(a, w, x)


def _ffn(x, P, layer, tag):
    h = _rmsnorm(x, P["norm_ffn"][layer], f"rms_ffn{layer}_{tag}")
    act, = _matmul(f"ffn_up{layer}_{tag}", [h], [P["w_g"][layer], P["w_u"][layer]], P["d_ff"],
                   [[(0, 0)], [(0, 1)]], _ep_swiglu, [BF16], tm=1024)
    return _deep_residual_matmul(f"ffn_down{layer}_{tag}", act, P["w_d"], layer, x)


def _trunk(x3, pos, past, conv_past, P, tag):
    B, T, D = x3.shape
    M = B * T
    H = N_HEADS
    W = WIDTH
    x = x3.reshape(M, D)
    one = [[(0, 0)]]
    tall = dict(tm=1024)
    wide = dict(tm=1024, tn=1024)

    h = _rmsnorm(x, P["norm_mix"][0], f"rms_mix0_{tag}")
    q, = _matmul(f"proj_q_{tag}", [h], [P["w_q"]], W, one,
                 functools.partial(_ep_headnorm, scales=(HEAD_DIM ** -0.5 * LOG2E,)), [BF16], full=[P["gq"]], **wide)
    k32, k16 = _matmul(f"proj_k_{tag}", [h], [P["w_k"]], W, one,
                       functools.partial(_ep_headnorm, scales=(1.0, 1.0)), [F32, BF16], full=[P["gk"]], **wide)
    v32, v16 = _matmul(f"proj_v_{tag}", [h], [P["w_v"]], W, one, _ep_identity, [F32, BF16], **wide)
    logf_pad, = _matmul(f"proj_f_{tag}", [h], [P["w_f"]], LANES, one, _ep_logsigmoid, [F32], col=[P["b_f"]],
                        tn=LANES)
    cos, sin = _rope_tables(pos)
    reps = max(1, wide["tm"] // T)
    cos, sin = jnp.tile(cos, (reps, 1)), jnp.tile(sin, (reps, 1))
    qk_b, = _matmul(f"proj_qkb_{tag}", [h], [P["w_qkb"]], 2 * W, one, _ep_rope, [BF16], col=[P["rope_scale"]],
                    row=[cos, sin], **wide)
    vg_b, = _matmul(f"proj_vgb_{tag}", [h], [P["w_vgb"]], 2 * W, one, _ep_identity, [F32], **wide)

    logf = logf_pad[:, :H].reshape(B, T, H)
    lf_t = logf.transpose(0, 2, 1).reshape(B * H, T)
    if past is None:
        _, c2_pieces = _cumsum_rows(lf_t, f"cumsum_{tag}", LOG2E)
        oa = _fox_prompt(q, k16, v16, c2_pieces, B, T)
        s0 = jnp.zeros((B, H, HEAD_DIM, HEAD_DIM), F32)
    else:
        kc, vc, lfc, s0 = past
        Pl = kc.shape[2]
        lf_all = jnp.concatenate([lfc.transpose(0, 2, 1).reshape(B * H, Pl), lf_t], axis=1)
        pad = (-lf_all.shape[1]) % LANES
        c2, _ = _cumsum_rows(jnp.pad(lf_all, ((0, 0), (0, pad))), f"cumsum_{tag}", LOG2E)
        oa = _fox_sample(q, k16, v16, kc, vc, c2[:, :Pl + T], B, T)
    ob, s_new = _retention(qk_b, vg_b, P["gn"], s0, B, T)
    x, = _matmul(f"out_proj_{tag}", [oa, ob], [P["w_oa"], P["w_ob"]], D, [[(0, 0), (1, 1)]], _ep_residual, [F32],
                 tile=[x], **wide)
    x = _ffn(x, P, 0, tag)

    h = _rmsnorm(x, P["norm_mix"][1], f"rms_mix1_{tag}")
    glu, = _matmul(f"conv_pw1_{tag}", [h], [P["w_1a"], P["w_1g"]], D, [[(0, 0)], [(0, 1)]], _ep_glu, [F32],
                   col=[P["b_1a"], P["b_1g"]], **tall)
    keep = CONV_WIDTH - 1
    assert T >= keep
    if conv_past is None:
        init = jnp.zeros((B, HALO, D), F32)
    else:
        init = jnp.pad(conv_past, ((0, 0), (HALO - keep, 0), (0, 0)))
    z = _conv_module(glu, init, P["w_dw"], P["b_dw"], P["ln_g"], P["ln_b"], B, T, tr=min(256, T))
    conv_new = glu.reshape(B, T, D)[:, -keep:]
    x, = _matmul(f"conv_pw2_{tag}", [z], [P["w_2"]], D, one, _ep_bias_residual, [F32], tile=[x], col=[P["b_2"]],
                 **wide)
    x = _ffn(x, P, 1, tag)

    return (x.reshape(B, T, D), k32.reshape(1, B, T, H, HEAD_DIM), v32.reshape(1, B, T, H, HEAD_DIM),
            logf[None], s_new[None], conv_new[None])


def kernel(x_prompt, x_sample, cache_k, cache_v, cache_logf, state_ret, state_conv,
           norm_mix, norm_ffn, w_in, b_forget, q_norm_gain, k_norm_gain, ret_norm_gain, w_out,
           w_pw1, b_pw1, w_dw, b_dw, conv_ln_gain, conv_ln_bias, w_pw2, b_pw2,
           w_ffn_gate, w_ffn_up, w_ffn_down):
    P = _prepare_weights(norm_mix, norm_ffn, w_in, b_forget, q_norm_gain, k_norm_gain, ret_norm_gain, w_out,
                         w_pw1, b_pw1, w_dw, b_dw, conv_ln_gain, conv_ln_bias, w_pw2, b_pw2,
                         w_ffn_gate, w_ffn_up, w_ffn_down)
    pos_p = jnp.arange(x_prompt.shape[1])
    y_p, k_p, v_p, lf_p, r_p, c_p = _trunk(x_prompt, pos_p, None, None, P, "p")
    _, Bs, Pl, H, hd = cache_k.shape
    pos_s = Pl + jnp.arange(x_sample.shape[1])
    past = (cache_k, cache_v, cache_logf.reshape(Bs, Pl, H), state_ret.reshape(Bs, H, hd, hd))
    conv_past = state_conv.reshape(state_conv.shape[1:])
    y_s, k_s, v_s, lf_s, r_s, c_s = _trunk(x_sample, pos_s, past, conv_past, P, "s")
    return (y_p, y_s, k_p, v_p, lf_p, r_p, c_p, k_s, v_s, lf_s, r_s, c_s)
```

```python
import functools
import math
from typing import NamedTuple, Optional

import jax
import jax.numpy as jnp
from jax import lax
from jax.experimental import pallas as pl
from jax.experimental.pallas import tpu as pltpu

F32 = jnp.float32
BF16 = jnp.bfloat16

LANES = 128
SUBLANES = 8
HALO = 32
VMEM_LIMIT = 56 * 1024 * 1024

HEAD_DIM = 128
N_HEADS = 8
WIDTH = N_HEADS * HEAD_DIM
CONV_WIDTH = 31
ROPE_BASE = 10000.0
EPS = 1e-6
NEG = -1e30
LOG2E = math.log2(math.e)
CONV_SPAN = (CONV_WIDTH - 1) // SUBLANES * SUBLANES
CONV_UNROLL = 5
RET_CHUNK = 256
RET_UNROLL = 4


def _params(n_grid):
    return pltpu.CompilerParams(dimension_semantics=("arbitrary",) * n_grid, vmem_limit_bytes=VMEM_LIMIT)


def _silu(x):
    return x * jax.nn.sigmoid(x)


def _dot(a, b):
    return jnp.dot(a, b, preferred_element_type=F32)


def _dot_nt(a, b):
    return lax.dot_general(a, b, (((1,), (1,)), ((), ())), preferred_element_type=F32)


def _rms_kernel(x_ref, g_ref, o_ref):
    x = x_ref[...]
    ms = jnp.mean(x * x, axis=-1, keepdims=True)
    o_ref[...] = (x * lax.rsqrt(ms + EPS) * g_ref[...]).astype(o_ref.dtype)


def _rmsnorm(x, g, name, tm=512):
    M, D = x.shape
    return pl.pallas_call(
        _rms_kernel,
        grid=(M // tm,),
        in_specs=[pl.BlockSpec((tm, D), lambda i: (i, 0)), pl.BlockSpec((1, D), lambda i: (0, 0))],
        out_specs=pl.BlockSpec((tm, D), lambda i: (i, 0)),
        out_shape=jax.ShapeDtypeStruct((M, D), BF16),
        compiler_params=_params(1),
        name=name,
    )(x, g.reshape(1, D))


class _Rhs(NamedTuple):
    arr: jax.Array
    lead: Optional[int]
    k: int
    k_blk: int = 0
    col: int = 0


def _mm_kernel(*refs, n_lhs, n_rhs, n_extra, n_out, products, epilogue):
    lhs = refs[:n_lhs]
    rhs = refs[n_lhs:n_lhs + n_rhs]
    extras = refs[n_lhs + n_rhs:n_lhs + n_rhs + n_extra]
    outs = refs[n_lhs + n_rhs + n_extra:n_lhs + n_rhs + n_extra + n_out]
    wbuf = refs[n_lhs + n_rhs + n_extra + n_out:]

    @pl.when(pl.program_id(1) == 0)
    def _():
        for w, b in zip(rhs, wbuf):
            b[...] = w[...].astype(BF16)

    accs = []
    for prod in products:
        acc = None
        for a, b in prod:
            d = _dot(lhs[a][...], wbuf[b][...])
            acc = d if acc is None else acc + d
        accs.append(acc)
    epilogue(accs, extras, outs)


def _matmul(name, lhs, rhs, n_cols, products, epilogue, out_dtypes, *, tile=(), col=(), row=(), full=(),
            tm=512, tn=512):
    M = lhs[0].shape[0]
    assert M % tm == 0 and n_cols % tn == 0
    in_specs = [pl.BlockSpec((tm, a.shape[1]), lambda j, i: (i, 0)) for a in lhs]
    for w in rhs:
        assert w.col % tn == 0
        c0 = w.col // tn
        if w.lead is None:
            in_specs.append(pl.BlockSpec((w.k, tn), lambda j, i, kb=w.k_blk, c0=c0: (kb, j + c0)))
        else:
            in_specs.append(pl.BlockSpec((None, w.k, tn),
                                         lambda j, i, ld=w.lead, kb=w.k_blk, c0=c0: (ld, kb, j + c0)))
    in_specs += [pl.BlockSpec((tm, tn), lambda j, i: (i, j)) for _ in tile]
    in_specs += [pl.BlockSpec((1, tn), lambda j, i: (0, j)) for _ in col]
    for r in row:
        assert r.shape[0] % tm == 0
        in_specs.append(pl.BlockSpec((tm, r.shape[1]), lambda j, i, nblk=r.shape[0] // tm: (i % nblk, 0)))
    in_specs += [pl.BlockSpec(f.shape, lambda j, i, nd=f.ndim: (0,) * nd) for f in full]
    extras = tuple(tile) + tuple(col) + tuple(row) + tuple(full)
    kern = functools.partial(_mm_kernel, n_lhs=len(lhs), n_rhs=len(rhs), n_extra=len(extras),
                             n_out=len(out_dtypes), products=products, epilogue=epilogue)
    return pl.pallas_call(
        kern,
        grid=(n_cols // tn, M // tm),
        in_specs=in_specs,
        out_specs=[pl.BlockSpec((tm, tn), lambda j, i: (i, j)) for _ in out_dtypes],
        out_shape=[jax.ShapeDtypeStruct((M, n_cols), dt) for dt in out_dtypes],
        scratch_shapes=[pltpu.VMEM((w.k, tn), BF16) for w in rhs],
        compiler_params=_params(2),
        name=name,
    )(*lhs, *[w.arr for w in rhs], *extras)


def _ep_headnorm(accs, extras, outs, *, scales):
    gain = extras[0][...]
    y = accs[0]
    for h in range(y.shape[1] // HEAD_DIM):
        sl = slice(h * HEAD_DIM, (h + 1) * HEAD_DIM)
        yh = y[:, sl]
        ms = jnp.mean(yh * yh, axis=-1, keepdims=True)
        r = yh * lax.rsqrt(ms + EPS) * gain
        for o, sc in zip(outs, scales):
            o[:, sl] = (r if sc == 1.0 else r * sc).astype(o.dtype)


def _ep_identity(accs, extras, outs):
    for o in outs:
        o[...] = accs[0].astype(o.dtype)


def _ep_logsigmoid(accs, extras, outs):
    z = accs[0] + extras[0][...]
    outs[0][...] = -(jnp.maximum(-z, 0.0) + jnp.log1p(jnp.exp(-jnp.abs(z))))


def _ep_rope(accs, extras, outs):
    scale, cos, sin = extras[0][...], extras[1][...], extras[2][...]
    y = accs[0]
    for h in range(y.shape[1] // HEAD_DIM):
        sl = slice(h * HEAD_DIM, (h + 1) * HEAD_DIM)
        yh = y[:, sl]
        r = yh * cos + pltpu.roll(yh, HEAD_DIM // 2, 1) * sin
        outs[0][:, sl] = (r * scale[:, sl]).astype(outs[0].dtype)


def _ep_residual(accs, extras, outs):
    outs[0][...] = extras[0][...] + accs[0]


def _ep_bias_residual(accs, extras, outs):
    outs[0][...] = extras[0][...] + (accs[0] + extras[1][...])


def _ep_swiglu(accs, extras, outs):
    outs[0][...] = (_silu(accs[0]) * accs[1]).astype(outs[0].dtype)


def _ep_glu(accs, extras, outs):
    a = accs[0] + extras[0][...]
    g = accs[1] + extras[1][...]
    outs[0][...] = a * jax.nn.sigmoid(g)


def _split3(x):
    hi = x.astype(BF16)
    r1 = x - hi.astype(F32)
    mid = r1.astype(BF16)
    lo = (r1 - mid.astype(F32)).astype(BF16)
    return hi, mid, lo


def _cumsum_kernel(x_ref, c_ref, hi_ref, mid_ref, lo_ref, *, scale):
    nblk, R, _ = x_ref.shape
    row = lax.broadcasted_iota(jnp.int32, (LANES, LANES), 0)
    col = lax.broadcasted_iota(jnp.int32, (LANES, LANES), 1)
    tri = jnp.where(row <= col, 1.0, 0.0).astype(BF16)
    ones = jnp.ones((LANES, LANES), BF16)

    def body(b, carry):
        hi, mid, lo = _split3(x_ref[b])
        within = _dot(hi, tri) + _dot(mid, tri) + _dot(lo, tri)
        total = _dot(hi, ones) + _dot(mid, ones) + _dot(lo, ones)
        c = (carry + within) * scale
        c_ref[b] = c
        hi_ref[b], mid_ref[b], lo_ref[b] = _split3(c)
        return carry + total

    lax.fori_loop(0, nblk, body, jnp.zeros((R, LANES), F32))


def _cumsum_rows(x, name, scale):
    R, N = x.shape
    nblk = N // LANES
    x3 = x.reshape(R, nblk, LANES).transpose(1, 0, 2)
    outs = pl.pallas_call(
        functools.partial(_cumsum_kernel, scale=scale),
        out_shape=[jax.ShapeDtypeStruct((nblk, R, LANES), dt) for dt in (F32, BF16, BF16, BF16)],
        compiler_params=pltpu.CompilerParams(vmem_limit_bytes=VMEM_LIMIT),
        name=name,
    )(x3)
    c, hi, mid, lo = [o.transpose(1, 0, 2).reshape(R, N) for o in outs]
    return c, (hi, mid, lo)


def _flash_step_t(st, vt, carry, mask):
    m, l, acc = carry
    if mask is not None:
        st = jnp.where(mask, st, NEG)
    m_new = jnp.maximum(m, jnp.max(st, axis=0, keepdims=True))
    alpha = jnp.exp2(m - m_new)
    pt = jnp.exp2(st - m_new)
    l = alpha * l + jnp.sum(pt, axis=0, keepdims=True)
    acc = alpha * acc + _dot(vt, pt.astype(BF16))
    return m_new, l, acc


def _flash_init_t(tq):
    return (jnp.full((1, tq), NEG, F32), jnp.zeros((1, tq), F32), jnp.zeros((HEAD_DIM, tq), F32))


def _causal_mask_t(tk, tq, q_off):
    keys = lax.broadcasted_iota(jnp.int32, (tk, tq), 0)
    queries = lax.broadcasted_iota(jnp.int32, (tk, tq), 1) + q_off
    return keys <= queries


def _fox_prompt_kernel(q_ref, qb_ref, k_ref, kb_ref, vt_ref, o_ref, *, tq, hb, qsplit):
    i = pl.program_id(2)
    tqc = tq // qsplit
    heads = [slice(h * HEAD_DIM, (h + 1) * HEAD_DIM) for h in range(hb)]
    chains = [(h, slice(c * tqc, (c + 1) * tqc)) for h in range(hb) for c in range(qsplit)]
    qs = [jnp.concatenate([q_ref[qr, heads[h]], qb_ref[h, qr, :]], axis=1) for h, qr in chains]

    def block(j, carries, diagonal):
        rows = pl.ds(pl.multiple_of(j * tq, tq), tq)
        kb = kb_ref[rows, :]
        ks = [jnp.concatenate([k_ref[rows, hs], kb], axis=1) for hs in heads]
        sts = [_dot_nt(ks[h], qs[n]) for n, (h, _) in enumerate(chains)]
        return tuple(
            _flash_step_t(sts[n], vt_ref[h, j], carries[n], _causal_mask_t(tq, tqc, qr.start) if diagonal else None)
            for n, (h, qr) in enumerate(chains))

    carries = lax.fori_loop(0, i, lambda j, c: block(j, c, False), (_flash_init_t(tqc),) * len(chains))
    carries = block(i, carries, True)
    for (m, l, acc), (h, qr) in zip(carries, chains):
        o_ref[qr, heads[h]] = (acc / l).T.astype(o_ref.dtype)


def _bias_columns(pieces, hb):
    R, S = pieces[0].shape
    parts = jnp.stack(pieces, axis=-1)
    ones = jnp.ones_like(parts)
    q6 = jnp.concatenate([parts, ones], axis=-1)
    k6 = jnp.concatenate([ones, -parts], axis=-1)
    slot = jax.nn.one_hot(jnp.arange(R) % hb, hb, dtype=q6.dtype)
    qb = (q6[:, :, None, :] * slot[:, None, :, None]).reshape(R, S, 6 * hb)
    kb = k6.reshape(R // hb, hb, S, 6).transpose(0, 2, 1, 3).reshape(R // hb, S, 6 * hb)
    pad = ((0, 0), (0, 0), (0, LANES - 6 * hb))
    return jnp.pad(qb, pad), jnp.pad(kb, pad)


def _fox_prompt(q, k, v, c2_pieces, B, S, tq=512, hb=4, qsplit=1):
    H = N_HEADS
    nq = S // tq
    nh = H // hb
    qb, kb = _bias_columns(c2_pieces, hb)
    vt = v.reshape(B, nq, tq, H, HEAD_DIM).transpose(0, 3, 1, 4, 2).reshape(B * H, nq, HEAD_DIM, tq)
    return pl.pallas_call(
        functools.partial(_fox_prompt_kernel, tq=tq, hb=hb, qsplit=qsplit),
        grid=(B, nh, nq),
        in_specs=[
            pl.BlockSpec((tq, hb * HEAD_DIM), lambda b, h, i: (b * nq + i, h)),
            pl.BlockSpec((hb, tq, LANES), lambda b, h, i: (b * nh + h, i, 0)),
            pl.BlockSpec((S, hb * HEAD_DIM), lambda b, h, i: (b, h)),
            pl.BlockSpec((None, S, LANES), lambda b, h, i: (b * nh + h, 0, 0)),
            pl.BlockSpec((hb, nq, HEAD_DIM, tq), lambda b, h, i: (b * nh + h, 0, 0, 0)),
        ],
        out_specs=pl.BlockSpec((tq, hb * HEAD_DIM), lambda b, h, i: (b * nq + i, h)),
        out_shape=jax.ShapeDtypeStruct((B * S, WIDTH), BF16),
        compiler_params=_params(3),
        name="fox_prompt",
    )(q, qb, k, kb, vt)


def _causal_mask(tq, tk):
    rows = lax.broadcasted_iota(jnp.int32, (tq, tk), 0)
    cols = lax.broadcasted_iota(jnp.int32, (tq, tk), 1)
    return cols <= rows


def _softmax_update(s, m, l):
    m_new = jnp.maximum(m, jnp.max(s, axis=-1, keepdims=True))
    alpha = jnp.exp2(m - m_new)
    p = jnp.exp2(s - m_new)
    return m_new, alpha * l + jnp.sum(p, axis=-1, keepdims=True), alpha, p


def _fox_sample_kernel(q_ref, kc_ref, vc_ref, kn_ref, vn_ref, cq_ref, ckc_ref, ckn_ref, o_ref, m_scr, l_scr, acc_scr):
    j = pl.program_id(1)
    T = q_ref.shape[0]
    heads = [slice(h * HEAD_DIM, (h + 1) * HEAD_DIM) for h in range(N_HEADS)]

    @pl.when(j == 0)
    def _():
        m_scr[...] = jnp.full(m_scr.shape, NEG, F32)
        l_scr[...] = jnp.zeros(l_scr.shape, F32)
        acc_scr[...] = jnp.zeros(acc_scr.shape, F32)

    k_all = pltpu.einshape("mhd->hmd", kc_ref[...])
    v_all = pltpu.einshape("mhd->hmd", vc_ref[...])
    for h, hs in enumerate(heads):
        s = _dot_nt(q_ref[:, hs], k_all[h].astype(BF16)) + (cq_ref[h] - ckc_ref[h])
        m, l, alpha, p = _softmax_update(s, m_scr[h], l_scr[h])
        acc_scr[h] = alpha * acc_scr[h] + _dot(p.astype(BF16), v_all[h].astype(BF16))
        m_scr[h], l_scr[h] = m, l

    @pl.when(j == pl.num_programs(1) - 1)
    def _():
        mask = _causal_mask(T, T)
        for h, hs in enumerate(heads):
            s = jnp.where(mask, _dot_nt(q_ref[:, hs], kn_ref[:, hs]) + (cq_ref[h] - ckn_ref[h]), NEG)
            _, l, alpha, p = _softmax_update(s, m_scr[h], l_scr[h])
            acc = alpha * acc_scr[h] + _dot(p.astype(BF16), vn_ref[:, hs])
            o_ref[:, hs] = (acc / l).astype(o_ref.dtype)


def _fox_sample(q, k_new, v_new, cache_k, cache_v, c2, B, T, tk=1024):
    H = N_HEADS
    P = cache_k.shape[2]
    nk = P // tk
    c_q = c2[:, P:].reshape(B, H, T, 1)
    c_kc = c2[:, :P].reshape(B, H, nk, 1, tk)
    c_kn = c2[:, P:].reshape(B, H, 1, T)
    tok = pl.BlockSpec((T, WIDTH), lambda b, j: (b, 0))
    cache = pl.BlockSpec((None, None, tk, H, HEAD_DIM), lambda b, j: (0, b, j, 0, 0))
    return pl.pallas_call(
        _fox_sample_kernel,
        grid=(B, nk),
        in_specs=[
            tok, cache, cache, tok, tok,
            pl.BlockSpec((None, H, T, 1), lambda b, j: (b, 0, 0, 0)),
            pl.BlockSpec((None, H, None, 1, tk), lambda b, j: (b, 0, j, 0, 0)),
            pl.BlockSpec((None, H, 1, T), lambda b, j: (b, 0, 0, 0)),
        ],
        out_specs=tok,
        out_shape=jax.ShapeDtypeStruct((B * T, WIDTH), BF16),
        scratch_shapes=[pltpu.VMEM((H, T, 1), F32), pltpu.VMEM((H, T, 1), F32), pltpu.VMEM((H, T, HEAD_DIM), F32)],
        compiler_params=_params(2),
        name="fox_sample",
    )(q, cache_k, cache_v, k_new, v_new, c_q, c_kc, c_kn)


def _retention_kernel(q_ref, k_ref, v_ref, g_ref, gn_ref, s0_ref, dm_ref, qd_ref, kd_ref, cd_ref,
                      o_ref, s_ref, *, L, hb, unroll):
    T = q_ref.shape[0]

    def chunk(off, h, s):
        rows = pl.ds(off, L)
        hs = slice(h * HEAD_DIM, (h + 1) * HEAD_DIM)
        q = q_ref[rows, hs]
        k = k_ref[rows, hs]
        v = v_ref[rows, hs].astype(BF16)
        att = _dot_nt(q, k) * dm_ref[h]
        o = _dot(att.astype(BF16), v) + _dot((q.astype(F32) * qd_ref[h]).astype(BF16), s.astype(BF16))
        kd = k.astype(F32) * kd_ref[h]
        s_new = s * cd_ref[h] + _dot(kd.T.astype(BF16), v)
        mu = jnp.mean(o, axis=-1, keepdims=True)
        oc = o - mu
        y = oc * lax.rsqrt(jnp.mean(oc * oc, axis=-1, keepdims=True) + EPS)
        o_ref[rows, hs] = (y * gn_ref[:, hs] * _silu(g_ref[rows, hs])).astype(o_ref.dtype)
        return s_new

    def body(it, states):
        states = list(states)
        for u in range(unroll):
            off = pl.multiple_of((it * unroll + u) * L, L)
            for h in range(hb):
                states[h] = chunk(off, h, states[h])
        return tuple(states)

    states = lax.fori_loop(0, T // (L * unroll), body, tuple(s0_ref[h] for h in range(hb)))
    for h in range(hb):
        s_ref[h] = states[h]


def _retention_consts(L):
    H = N_HEADS
    log_g = jnp.log1p(-jnp.exp2(-5.0 - jnp.arange(H, dtype=F32)))
    i = jnp.arange(L, dtype=F32)
    diff = i[:, None] - i[None, :]
    dmat = jnp.where(diff[None] >= 0, jnp.exp(jnp.maximum(diff, 0.0)[None] * log_g[:, None, None]), 0.0)
    q_dec = jnp.exp((i + 1.0)[None, :] * log_g[:, None])
    k_dec = jnp.exp((L - 1.0 - i)[None, :] * log_g[:, None])
    c_dec = jnp.exp(L * log_g)
    rep = lambda a: jnp.broadcast_to(a[..., None], a.shape + (LANES,))
    return dmat, rep(q_dec), rep(k_dec), rep(c_dec[:, None])


def _retention(qk, vg, gn, s0, B, T):
    H = N_HEADS
    L = min(RET_CHUNK, T)
    n_chunks = T // L
    unroll = math.gcd(RET_UNROLL, n_chunks)
    hb = 1 if n_chunks > 1 else H
    nh = H // hb
    dmat, q_dec, k_dec, c_dec = _retention_consts(L)
    seq = lambda off: pl.BlockSpec((T, hb * HEAD_DIM), lambda b, h: (b, h + off))
    per_head = lambda r: pl.BlockSpec((hb, r, LANES), lambda b, h: (h, 0, 0))
    state = pl.BlockSpec((None, hb, HEAD_DIM, HEAD_DIM), lambda b, h: (b, h, 0, 0))
    return pl.pallas_call(
        functools.partial(_retention_kernel, L=L, hb=hb, unroll=unroll),
        grid=(B, nh),
        in_specs=[seq(0), seq(nh), seq(0), seq(nh),
                  pl.BlockSpec((1, hb * HEAD_DIM), lambda b, h: (0, h)), state,
                  pl.BlockSpec((hb, L, L), lambda b, h: (h, 0, 0)), per_head(L), per_head(L), per_head(1)],
        out_specs=[seq(0), state],
        out_shape=[jax.ShapeDtypeStruct((B * T, WIDTH), BF16), jax.ShapeDtypeStruct((B, H, HEAD_DIM, HEAD_DIM), F32)],
        compiler_params=_params(2),
        name="retention",
    )(qk, qk, vg, vg, gn.reshape(1, WIDTH), s0, dmat, q_dec, k_dec, c_dec)


def _conv_kernel(x_ref, prev_ref, init_ref, w_ref, b_ref, g_ref, beta_ref, o_ref, xp_ref, xs_ref, z_ref, *, tr):
    i = pl.program_id(1)

    @pl.when(i == 0)
    def _():
        xp_ref[0:HALO, :] = init_ref[...]

    @pl.when(i > 0)
    def _():
        xp_ref[0:HALO, :] = prev_ref[...]

    C = x_ref.shape[1]
    xp_ref[HALO:HALO + tr, :] = x_ref[...]
    xp_ref[HALO + tr:HALO + tr + SUBLANES, :] = jnp.zeros((SUBLANES, C), F32)
    first = HALO - (CONV_WIDTH - 1)
    span = tr + CONV_SPAN
    for r in range(SUBLANES):
        xs_ref[r] = xp_ref[pl.ds(first + r, span), :]

    groups = CONV_SPAN // SUBLANES + 1
    for c in range(C // LANES):
        cs = slice(c * LANES, (c + 1) * LANES)
        wts = [w_ref[w, :, cs] for w in range(CONV_WIDTH)]
        bias = jnp.broadcast_to(b_ref[:, cs], (SUBLANES, LANES))

        def step(u, accs, cs=cs, wts=wts, bias=bias):
            accs = (bias,) + accs
            r0 = pl.multiple_of(u * SUBLANES, SUBLANES)
            for r in range(SUBLANES):
                x = xs_ref[r, pl.ds(r0, SUBLANES), cs]
                accs = tuple(acc + x * wts[SUBLANES * a + r] if SUBLANES * a + r < CONV_WIDTH else acc
                             for a, acc in enumerate(accs))
            z_ref[pl.ds(r0, SUBLANES), cs] = accs[-1]
            return accs[:-1]

        lax.fori_loop(0, span // SUBLANES, step, (bias,) * (groups - 1), unroll=CONV_UNROLL)

    z = z_ref[CONV_SPAN:CONV_SPAN + tr, :]
    mu = jnp.mean(z, axis=-1, keepdims=True)
    zc = z - mu
    y = zc * lax.rsqrt(jnp.mean(zc * zc, axis=-1, keepdims=True) + EPS) * g_ref[...] + beta_ref[...]
    o_ref[...] = _silu(y).astype(o_ref.dtype)


def _conv_module(glu, init, w_dw, b_dw, ln_g, ln_b, B, T, tr):
    C = glu.shape[1]
    nt = T // tr
    per = tr // HALO
    w_rep = jnp.broadcast_to(jnp.pad(w_dw, ((0, HALO - CONV_WIDTH), (0, 0)))[:, None, :], (HALO, SUBLANES, C))
    vec = pl.BlockSpec((1, C), lambda b, i: (0, 0))
    return pl.pallas_call(
        functools.partial(_conv_kernel, tr=tr),
        grid=(B, nt),
        in_specs=[
            pl.BlockSpec((tr, C), lambda b, i: (b * nt + i, 0)),
            pl.BlockSpec((HALO, C), lambda b, i: (b * nt * per + jnp.maximum(i * per - 1, 0), 0)),
            pl.BlockSpec((None, HALO, C), lambda b, i: (b, 0, 0)),
            pl.BlockSpec((HALO, SUBLANES, C), lambda b, i: (0, 0, 0)),
            vec, vec, vec,
        ],
        out_specs=pl.BlockSpec((tr, C), lambda b, i: (b * nt + i, 0)),
        out_shape=jax.ShapeDtypeStruct((B * T, C), BF16),
        scratch_shapes=[pltpu.VMEM((HALO + tr + SUBLANES, C), F32),
                        pltpu.VMEM((SUBLANES, tr + CONV_SPAN, C), F32),
                        pltpu.VMEM((tr + CONV_SPAN, C), F32)],
        compiler_params=_params(2),
        name="conv_module",
    )(glu, glu, init, w_rep, b_dw.reshape(1, C), ln_g.reshape(1, C), ln_b.reshape(1, C))


def _rope_tables(pos):
    half = HEAD_DIM // 2
    inv = jnp.exp(-math.log(ROPE_BASE) * jnp.arange(half, dtype=F32) / half)
    ang = pos.astype(F32)[:, None] * inv[None, :]
    cos, sin = jnp.cos(ang), jnp.sin(ang)
    return jnp.concatenate([cos, cos], axis=-1), jnp.concatenate([-sin, sin], axis=-1)


def _prepare_weights(norm_mix, norm_ffn, w_in, b_forget, q_norm_gain, k_norm_gain, ret_norm_gain, w_out,
                     w_pw1, b_pw1, w_dw, b_dw, conv_ln_gain, conv_ln_bias, w_pw2, b_pw2,
                     w_ffn_gate, w_ffn_up, w_ffn_down):
    W = WIDTH
    D = w_in.shape[1]
    f0 = 3 * W
    f1 = f0 + N_HEADS
    w_b = w_in[0, :, f1:]
    return dict(
        norm_mix=norm_mix, norm_ffn=norm_ffn,
        w_q=_Rhs(w_in, 0, D, col=0), w_k=_Rhs(w_in, 0, D, col=W), w_v=_Rhs(w_in, 0, D, col=2 * W),
        w_f=_Rhs(w_in, 0, D, col=f0),
        b_f=jnp.pad(b_forget[0], (0, LANES - N_HEADS)).reshape(1, LANES),
        w_qkb=_Rhs(w_b, None, D, col=0), w_vgb=_Rhs(w_b, None, D, col=2 * W),
        rope_scale=jnp.concatenate([jnp.ones((1, W), F32), jnp.full((1, W), HEAD_DIM ** -0.5, F32)], axis=1),
        gq=q_norm_gain[0].reshape(1, HEAD_DIM), gk=k_norm_gain[0].reshape(1, HEAD_DIM), gn=ret_norm_gain[0],
        w_oa=_Rhs(w_out, 0, W, k_blk=0), w_ob=_Rhs(w_out, 0, W, k_blk=1),
        w_1a=_Rhs(w_pw1, 0, D, col=0), w_1g=_Rhs(w_pw1, 0, D, col=w_pw1.shape[2] // 2),
        b_1a=b_pw1[0, :2 * W].reshape(1, -1), b_1g=b_pw1[0, 2 * W:].reshape(1, -1),
        w_dw=w_dw[0], b_dw=b_dw[0], ln_g=conv_ln_gain[0], ln_b=conv_ln_bias[0],
        w_2=_Rhs(w_pw2, 0, w_pw2.shape[1]), b_2=b_pw2[0].reshape(1, -1),
        w_g=[_Rhs(w_ffn_gate, l, D) for l in range(2)], w_u=[_Rhs(w_ffn_up, l, D) for l in range(2)],
        w_d=[_Rhs(w_ffn_down, l, w_ffn_down.shape[1]) for l in range(2)],
        d_ff=w_ffn_gate.shape[2],
    )


def _ffn(x, P, layer, tag):
    D = x.shape[1]
    h = _rmsnorm(x, P["norm_ffn"][layer], f"rms_ffn{layer}_{tag}")
    act, = _matmul(f"ffn_up{layer}_{tag}", [h], [P["w_g"][layer], P["w_u"][layer]], P["d_ff"],
                   [[(0, 0)], [(0, 1)]], _ep_swiglu, [BF16], tm=1024)
    out, = _matmul(f"ffn_down{layer}_{tag}", [act], [P["w_d"][layer]], D, [[(0, 0)]], _ep_residual, [F32],
                   tile=[x])
    return out


def _trunk(x3, pos, past, conv_past, P, tag):
    B, T, D = x3.shape
    M = B * T
    H = N_HEADS
    W = WIDTH
    x = x3.reshape(M, D)
    one = [[(0, 0)]]
    tall = dict(tm=1024)
    wide = dict(tm=1024, tn=1024)

    h = _rmsnorm(x, P["norm_mix"][0], f"rms_mix0_{tag}")
    q, = _matmul(f"proj_q_{tag}", [h], [P["w_q"]], W, one,
                 functools.partial(_ep_headnorm, scales=(HEAD_DIM ** -0.5 * LOG2E,)), [BF16], full=[P["gq"]], **wide)
    k32, k16 = _matmul(f"proj_k_{tag}", [h], [P["w_k"]], W, one,
                       functools.partial(_ep_headnorm, scales=(1.0, 1.0)), [F32, BF16], full=[P["gk"]], **wide)
    v32, v16 = _matmul(f"proj_v_{tag}", [h], [P["w_v"]], W, one, _ep_identity, [F32, BF16], **wide)
    logf_pad, = _matmul(f"proj_f_{tag}", [h], [P["w_f"]], LANES, one, _ep_logsigmoid, [F32], col=[P["b_f"]],
                        tn=LANES)
    cos, sin = _rope_tables(pos)
    reps = max(1, wide["tm"] // T)
    cos, sin = jnp.tile(cos, (reps, 1)), jnp.tile(sin, (reps, 1))
    qk_b, = _matmul(f"proj_qkb_{tag}", [h], [P["w_qkb"]], 2 * W, one, _ep_rope, [BF16], col=[P["rope_scale"]],
                    row=[cos, sin], **wide)
    vg_b, = _matmul(f"proj_vgb_{tag}", [h], [P["w_vgb"]], 2 * W, one, _ep_identity, [F32], **wide)

    logf = logf_pad[:, :H].reshape(B, T, H)
    lf_t = logf.transpose(0, 2, 1).reshape(B * H, T)
    if past is None:
        _, c2_pieces = _cumsum_rows(lf_t, f"cumsum_{tag}", LOG2E)
        oa = _fox_prompt(q, k16, v16, c2_pieces, B, T)
        s0 = jnp.zeros((B, H, HEAD_DIM, HEAD_DIM), F32)
    else:
        kc, vc, lfc, s0 = past
        Pl = kc.shape[2]
        lf_all = jnp.concatenate([lfc.transpose(0, 2, 1).reshape(B * H, Pl), lf_t], axis=1)
        pad = (-lf_all.shape[1]) % LANES
        c2, _ = _cumsum_rows(jnp.pad(lf_all, ((0, 0), (0, pad))), f"cumsum_{tag}", LOG2E)
        oa = _fox_sample(q, k16, v16, kc, vc, c2[:, :Pl + T], B, T)
    ob, s_new = _retention(qk_b, vg_b, P["gn"], s0, B, T)
    x, = _matmul(f"out_proj_{tag}", [oa, ob], [P["w_oa"], P["w_ob"]], D, [[(0, 0), (1, 1)]], _ep_residual, [F32],
                 tile=[x], **wide)
    x = _ffn(x, P, 0, tag)

    h = _rmsnorm(x, P["norm_mix"][1], f"rms_mix1_{tag}")
    glu, = _matmul(f"conv_pw1_{tag}", [h], [P["w_1a"], P["w_1g"]], D, [[(0, 0)], [(0, 1)]], _ep_glu, [F32],
                   col=[P["b_1a"], P["b_1g"]], **tall)
    keep = CONV_WIDTH - 1
    assert T >= keep
    if conv_past is None:
        init = jnp.zeros((B, HALO, D), F32)
    else:
        init = jnp.pad(conv_past, ((0, 0), (HALO - keep, 0), (0, 0)))
    z = _conv_module(glu, init, P["w_dw"], P["b_dw"], P["ln_g"], P["ln_b"], B, T, tr=min(256, T))
    conv_new = glu.reshape(B, T, D)[:, -keep:]
    x, = _matmul(f"conv_pw2_{tag}", [z], [P["w_2"]], D, one, _ep_bias_residual, [F32], tile=[x], col=[P["b_2"]],
                 **wide)
    x = _ffn(x, P, 1, tag)

    return (x.reshape(B, T, D), k32.reshape(1, B, T, H, HEAD_DIM), v32.reshape(1, B, T, H, HEAD_DIM),
            logf[None], s_new[None], conv_new[None])


def kernel(x_prompt, x_sample, cache_k, cache_v, cache_logf, state_ret, state_conv,
           norm_mix, norm_ffn, w_in, b_forget, q_norm_gain, k_norm_gain, ret_norm_gain, w_out,
           w_pw1, b_pw1, w_dw, b_dw, conv_ln_gain, conv_ln_bias, w_pw2, b_pw2,
           w_ffn_gate, w_ffn_up, w_ffn_down):
    P = _prepare_weights(norm_mix, norm_ffn, w_in, b_forget, q_norm_gain, k_norm_gain, ret_norm_gain, w_out,
                         w_pw1, b_pw1, w_dw, b_dw, conv_ln_gain, conv_ln_bias, w_pw2, b_pw2,
                         w_ffn_gate, w_ffn_up, w_ffn_down)
    pos_p = jnp.arange(x_prompt.shape[1])
    y_p, k_p, v_p, lf_p, r_p, c_p = _trunk(x_prompt, pos_p, None, None, P, "p")
    _, Bs, Pl, H, hd = cache_k.shape
    pos_s = Pl + jnp.arange(x_sample.shape[1])
    past = (cache_k, cache_v, cache_logf.reshape(Bs, Pl, H), state_ret.reshape(Bs, H, hd, hd))
    conv_past = state_conv.reshape(state_conv.shape[1:])
    y_s, k_s, v_s, lf_s, r_s, c_s = _trunk(x_sample, pos_s, past, conv_past, P, "s")
    return (y_p, y_s, k_p, v_p, lf_p, r_p, c_p, k_s, v_s, lf_s, r_s, c_s)
```

```python
import functools
import math
from typing import NamedTuple, Optional

import jax
import jax.numpy as jnp
from jax import lax
from jax.experimental import pallas as pl
from jax.experimental.pallas import tpu as pltpu

F32 = jnp.float32
BF16 = jnp.bfloat16

LANES = 128
SUBLANES = 8
HALO = 32
VMEM_LIMIT = 56 * 1024 * 1024

HEAD_DIM = 128
N_HEADS = 8
WIDTH = N_HEADS * HEAD_DIM
CONV_WIDTH = 31
ROPE_BASE = 10000.0
EPS = 1e-6
NEG = -1e30
LOG2E = math.log2(math.e)
CONV_SPAN = (CONV_WIDTH - 1) // SUBLANES * SUBLANES
CONV_UNROLL = 5
RET_CHUNK = 256
MM_ROW_SPLIT = 4
RET_UNROLL = 4


def _params(n_grid):
    return pltpu.CompilerParams(dimension_semantics=("arbitrary",) * n_grid, vmem_limit_bytes=VMEM_LIMIT)


def _silu(x):
    return x * jax.nn.sigmoid(x)


def _dot(a, b):
    return jnp.dot(a, b, preferred_element_type=F32)


def _dot_nt(a, b):
    return lax.dot_general(a, b, (((1,), (1,)), ((), ())), preferred_element_type=F32)


def _rms_kernel(x_ref, g_ref, o_ref):
    x = x_ref[...]
    ms = jnp.mean(x * x, axis=-1, keepdims=True)
    o_ref[...] = (x * lax.rsqrt(ms + EPS) * g_ref[...]).astype(o_ref.dtype)


def _rmsnorm(x, g, name, tm=512):
    M, D = x.shape
    return pl.pallas_call(
        _rms_kernel,
        grid=(M // tm,),
        in_specs=[pl.BlockSpec((tm, D), lambda i: (i, 0)), pl.BlockSpec((1, D), lambda i: (0, 0))],
        out_specs=pl.BlockSpec((tm, D), lambda i: (i, 0)),
        out_shape=jax.ShapeDtypeStruct((M, D), BF16),
        compiler_params=_params(1),
        name=name,
    )(x, g.reshape(1, D))


class _Rhs(NamedTuple):
    arr: jax.Array
    lead: Optional[int]
    k: int
    k_blk: int = 0
    col: int = 0


def _mm_kernel(*refs, n_lhs, n_rhs, n_extra, n_out, row_blocked, products, epilogue):
    lhs = refs[:n_lhs]
    rhs = refs[n_lhs:n_lhs + n_rhs]
    extras = refs[n_lhs + n_rhs:n_lhs + n_rhs + n_extra]
    outs = refs[n_lhs + n_rhs + n_extra:n_lhs + n_rhs + n_extra + n_out]
    wbuf = refs[n_lhs + n_rhs + n_extra + n_out:]

    @pl.when(pl.program_id(1) == 0)
    def _():
        for w, b in zip(rhs, wbuf):
            b[...] = w[...].astype(BF16)

    ts = lhs[0].shape[0] // MM_ROW_SPLIT

    def slab_dots(s):
        rows = pl.ds(s * ts, ts)
        accs = []
        for prod in products:
            acc = None
            for a, b in prod:
                d = _dot(lhs[a][rows, :], wbuf[b][...])
                acc = d if acc is None else acc + d
            accs.append(acc)
        return accs

    def slab_epilogue(s, accs):
        rows = pl.ds(s * ts, ts)
        epilogue(accs, [e.at[rows, :] if blocked else e for e, blocked in zip(extras, row_blocked)],
                 [o.at[rows, :] for o in outs])

    pending = slab_dots(0)
    for s in range(1, MM_ROW_SPLIT):
        nxt = slab_dots(s)
        slab_epilogue(s - 1, pending)
        pending = nxt
    slab_epilogue(MM_ROW_SPLIT - 1, pending)


def _matmul(name, lhs, rhs, n_cols, products, epilogue, out_dtypes, *, tile=(), col=(), row=(), full=(),
            tm=512, tn=512):
    M = lhs[0].shape[0]
    assert M % tm == 0 and n_cols % tn == 0
    in_specs = [pl.BlockSpec((tm, a.shape[1]), lambda j, i: (i, 0)) for a in lhs]
    for w in rhs:
        assert w.col % tn == 0
        c0 = w.col // tn
        if w.lead is None:
            in_specs.append(pl.BlockSpec((w.k, tn), lambda j, i, kb=w.k_blk, c0=c0: (kb, j + c0)))
        else:
            in_specs.append(pl.BlockSpec((None, w.k, tn),
                                         lambda j, i, ld=w.lead, kb=w.k_blk, c0=c0: (ld, kb, j + c0)))
    in_specs += [pl.BlockSpec((tm, tn), lambda j, i: (i, j)) for _ in tile]
    in_specs += [pl.BlockSpec((1, tn), lambda j, i: (0, j)) for _ in col]
    for r in row:
        assert r.shape[0] % tm == 0
        in_specs.append(pl.BlockSpec((tm, r.shape[1]), lambda j, i, nblk=r.shape[0] // tm: (i % nblk, 0)))
    in_specs += [pl.BlockSpec(f.shape, lambda j, i, nd=f.ndim: (0,) * nd) for f in full]
    extras = tuple(tile) + tuple(col) + tuple(row) + tuple(full)
    row_blocked = (True,) * len(tile) + (False,) * len(col) + (True,) * len(row) + (False,) * len(full)
    kern = functools.partial(_mm_kernel, n_lhs=len(lhs), n_rhs=len(rhs), n_extra=len(extras),
                             n_out=len(out_dtypes), row_blocked=row_blocked, products=products, epilogue=epilogue)
    return pl.pallas_call(
        kern,
        grid=(n_cols // tn, M // tm),
        in_specs=in_specs,
        out_specs=[pl.BlockSpec((tm, tn), lambda j, i: (i, j)) for _ in out_dtypes],
        out_shape=[jax.ShapeDtypeStruct((M, n_cols), dt) for dt in out_dtypes],
        scratch_shapes=[pltpu.VMEM((w.k, tn), BF16) for w in rhs],
        compiler_params=_params(2),
        name=name,
    )(*lhs, *[w.arr for w in rhs], *extras)


def _ep_headnorm(accs, extras, outs, *, scales):
    gain = extras[0][...]
    y = accs[0]
    for h in range(y.shape[1] // HEAD_DIM):
        sl = slice(h * HEAD_DIM, (h + 1) * HEAD_DIM)
        yh = y[:, sl]
        ms = jnp.mean(yh * yh, axis=-1, keepdims=True)
        r = yh * lax.rsqrt(ms + EPS) * gain
        for o, sc in zip(outs, scales):
            o[:, sl] = (r if sc == 1.0 else r * sc).astype(o.dtype)


def _ep_identity(accs, extras, outs):
    for o in outs:
        o[...] = accs[0].astype(o.dtype)


def _ep_logsigmoid(accs, extras, outs):
    z = accs[0] + extras[0][...]
    outs[0][...] = -(jnp.maximum(-z, 0.0) + jnp.log1p(jnp.exp(-jnp.abs(z))))


def _ep_rope(accs, extras, outs):
    scale, cos, sin = extras[0][...], extras[1][...], extras[2][...]
    y = accs[0]
    for h in range(y.shape[1] // HEAD_DIM):
        sl = slice(h * HEAD_DIM, (h + 1) * HEAD_DIM)
        yh = y[:, sl]
        r = yh * cos + pltpu.roll(yh, HEAD_DIM // 2, 1) * sin
        outs[0][:, sl] = (r * scale[:, sl]).astype(outs[0].dtype)


def _ep_residual(accs, extras, outs):
    outs[0][...] = extras[0][...] + accs[0]


def _ep_bias_residual(accs, extras, outs):
    outs[0][...] = extras[0][...] + (accs[0] + extras[1][...])


def _ep_swiglu(accs, extras, outs):
    outs[0][...] = (_silu(accs[0]) * accs[1]).astype(outs[0].dtype)


def _ep_glu(accs, extras, outs):
    a = accs[0] + extras[0][...]
    g = accs[1] + extras[1][...]
    outs[0][...] = a * jax.nn.sigmoid(g)


def _split3(x):
    hi = x.astype(BF16)
    r1 = x - hi.astype(F32)
    mid = r1.astype(BF16)
    lo = (r1 - mid.astype(F32)).astype(BF16)
    return hi, mid, lo


def _cumsum_kernel(x_ref, c_ref, hi_ref, mid_ref, lo_ref, *, scale):
    nblk, R, _ = x_ref.shape
    row = lax.broadcasted_iota(jnp.int32, (LANES, LANES), 0)
    col = lax.broadcasted_iota(jnp.int32, (LANES, LANES), 1)
    tri = jnp.where(row <= col, 1.0, 0.0).astype(BF16)
    ones = jnp.ones((LANES, LANES), BF16)

    def body(b, carry):
        hi, mid, lo = _split3(x_ref[b])
        within = _dot(hi, tri) + _dot(mid, tri) + _dot(lo, tri)
        total = _dot(hi, ones) + _dot(mid, ones) + _dot(lo, ones)
        c = (carry + within) * scale
        c_ref[b] = c
        hi_ref[b], mid_ref[b], lo_ref[b] = _split3(c)
        return carry + total

    lax.fori_loop(0, nblk, body, jnp.zeros((R, LANES), F32))


def _cumsum_rows(x, name, scale):
    R, N = x.shape
    nblk = N // LANES
    x3 = x.reshape(R, nblk, LANES).transpose(1, 0, 2)
    outs = pl.pallas_call(
        functools.partial(_cumsum_kernel, scale=scale),
        out_shape=[jax.ShapeDtypeStruct((nblk, R, LANES), dt) for dt in (F32, BF16, BF16, BF16)],
        compiler_params=pltpu.CompilerParams(vmem_limit_bytes=VMEM_LIMIT),
        name=name,
    )(x3)
    c, hi, mid, lo = [o.transpose(1, 0, 2).reshape(R, N) for o in outs]
    return c, (hi, mid, lo)


def _flash_step_t(st, vt, carry, mask):
    m, l, acc = carry
    if mask is not None:
        st = jnp.where(mask, st, NEG)
    m_new = jnp.maximum(m, jnp.max(st, axis=0, keepdims=True))
    alpha = jnp.exp2(m - m_new)
    pt = jnp.exp2(st - m_new)
    l = alpha * l + jnp.sum(pt, axis=0, keepdims=True)
    acc = alpha * acc + _dot(vt, pt.astype(BF16))
    return m_new, l, acc


def _flash_init_t(tq):
    return (jnp.full((1, tq), NEG, F32), jnp.zeros((1, tq), F32), jnp.zeros((HEAD_DIM, tq), F32))


def _causal_mask_t(tk, tq, q_off):
    keys = lax.broadcasted_iota(jnp.int32, (tk, tq), 0)
    queries = lax.broadcasted_iota(jnp.int32, (tk, tq), 1) + q_off
    return keys <= queries


def _fox_prompt_kernel(q_ref, qb_ref, k_ref, kb_ref, vt_ref, o_ref, *, tq, hb, qsplit):
    i = pl.program_id(2)
    tqc = tq // qsplit
    heads = [slice(h * HEAD_DIM, (h + 1) * HEAD_DIM) for h in range(hb)]
    chains = [(h, slice(c * tqc, (c + 1) * tqc)) for h in range(hb) for c in range(qsplit)]
    qs = [jnp.concatenate([q_ref[qr, heads[h]], qb_ref[h, qr, :]], axis=1) for h, qr in chains]

    def block(j, carries, diagonal):
        rows = pl.ds(pl.multiple_of(j * tq, tq), tq)
        kb = kb_ref[rows, :]
        ks = [jnp.concatenate([k_ref[rows, hs], kb], axis=1) for hs in heads]
        sts = [_dot_nt(ks[h], qs[n]) for n, (h, _) in enumerate(chains)]
        return tuple(
            _flash_step_t(sts[n], vt_ref[h, j], carries[n], _causal_mask_t(tq, tqc, qr.start) if diagonal else None)
            for n, (h, qr) in enumerate(chains))

    carries = lax.fori_loop(0, i, lambda j, c: block(j, c, False), (_flash_init_t(tqc),) * len(chains))
    carries = block(i, carries, True)
    for (m, l, acc), (h, qr) in zip(carries, chains):
        o_ref[qr, heads[h]] = (acc / l).T.astype(o_ref.dtype)


def _bias_columns(pieces, hb):
    R, S = pieces[0].shape
    parts = jnp.stack(pieces, axis=-1)
    ones = jnp.ones_like(parts)
    q6 = jnp.concatenate([parts, ones], axis=-1)
    k6 = jnp.concatenate([ones, -parts], axis=-1)
    slot = jax.nn.one_hot(jnp.arange(R) % hb, hb, dtype=q6.dtype)
    qb = (q6[:, :, None, :] * slot[:, None, :, None]).reshape(R, S, 6 * hb)
    kb = k6.reshape(R // hb, hb, S, 6).transpose(0, 2, 1, 3).reshape(R // hb, S, 6 * hb)
    pad = ((0, 0), (0, 0), (0, LANES - 6 * hb))
    return jnp.pad(qb, pad), jnp.pad(kb, pad)


def _fox_prompt(q, k, v, c2_pieces, B, S, tq=512, hb=4, qsplit=1):
    H = N_HEADS
    nq = S // tq
    nh = H // hb
    qb, kb = _bias_columns(c2_pieces, hb)
    vt = v.reshape(B, nq, tq, H, HEAD_DIM).transpose(0, 3, 1, 4, 2).reshape(B * H, nq, HEAD_DIM, tq)
    return pl.pallas_call(
        functools.partial(_fox_prompt_kernel, tq=tq, hb=hb, qsplit=qsplit),
        grid=(B, nh, nq),
        in_specs=[
            pl.BlockSpec((tq, hb * HEAD_DIM), lambda b, h, i: (b * nq + i, h)),
            pl.BlockSpec((hb, tq, LANES), lambda b, h, i: (b * nh + h, i, 0)),
            pl.BlockSpec((S, hb * HEAD_DIM), lambda b, h, i: (b, h)),
            pl.BlockSpec((None, S, LANES), lambda b, h, i: (b * nh + h, 0, 0)),
            pl.BlockSpec((hb, nq, HEAD_DIM, tq), lambda b, h, i: (b * nh + h, 0, 0, 0)),
        ],
        out_specs=pl.BlockSpec((tq, hb * HEAD_DIM), lambda b, h, i: (b * nq + i, h)),
        out_shape=jax.ShapeDtypeStruct((B * S, WIDTH), BF16),
        compiler_params=_params(3),
        name="fox_prompt",
    )(q, qb, k, kb, vt)


def _causal_mask(tq, tk):
    rows = lax.broadcasted_iota(jnp.int32, (tq, tk), 0)
    cols = lax.broadcasted_iota(jnp.int32, (tq, tk), 1)
    return cols <= rows


def _softmax_update(s, m, l):
    m_new = jnp.maximum(m, jnp.max(s, axis=-1, keepdims=True))
    alpha = jnp.exp2(m - m_new)
    p = jnp.exp2(s - m_new)
    return m_new, alpha * l + jnp.sum(p, axis=-1, keepdims=True), alpha, p


def _fox_sample_kernel(q_ref, kc_ref, vc_ref, kn_ref, vn_ref, cq_ref, ckc_ref, ckn_ref, o_ref, m_scr, l_scr, acc_scr):
    j = pl.program_id(1)
    T = q_ref.shape[0]
    heads = [slice(h * HEAD_DIM, (h + 1) * HEAD_DIM) for h in range(N_HEADS)]

    @pl.when(j == 0)
    def _():
        m_scr[...] = jnp.full(m_scr.shape, NEG, F32)
        l_scr[...] = jnp.zeros(l_scr.shape, F32)
        acc_scr[...] = jnp.zeros(acc_scr.shape, F32)

    k_all = pltpu.einshape("mhd->hmd", kc_ref[...])
    v_all = pltpu.einshape("mhd->hmd", vc_ref[...])
    scores = [_dot_nt(q_ref[:, hs], k_all[h].astype(BF16)) + (cq_ref[h] - ckc_ref[h]) for h, hs in enumerate(heads)]
    stats = [_softmax_update(scores[h], m_scr[h], l_scr[h]) for h in range(N_HEADS)]
    for h, (m, l, alpha, p) in enumerate(stats):
        acc_scr[h] = alpha * acc_scr[h] + _dot(p.astype(BF16), v_all[h].astype(BF16))
        m_scr[h], l_scr[h] = m, l

    @pl.when(j == pl.num_programs(1) - 1)
    def _():
        mask = _causal_mask(T, T)
        for h, hs in enumerate(heads):
            s = jnp.where(mask, _dot_nt(q_ref[:, hs], kn_ref[:, hs]) + (cq_ref[h] - ckn_ref[h]), NEG)
            _, l, alpha, p = _softmax_update(s, m_scr[h], l_scr[h])
            acc = alpha * acc_scr[h] + _dot(p.astype(BF16), vn_ref[:, hs])
            o_ref[:, hs] = (acc / l).astype(o_ref.dtype)


def _fox_sample(q, k_new, v_new, cache_k, cache_v, c2, B, T, tk=1024):
    H = N_HEADS
    P = cache_k.shape[2]
    nk = P // tk
    c_q = c2[:, P:].reshape(B, H, T, 1)
    c_kc = c2[:, :P].reshape(B, H, nk, 1, tk)
    c_kn = c2[:, P:].reshape(B, H, 1, T)
    tok = pl.BlockSpec((T, WIDTH), lambda b, j: (b, 0))
    cache = pl.BlockSpec((None, None, tk, H, HEAD_DIM), lambda b, j: (0, b, j, 0, 0))
    return pl.pallas_call(
        _fox_sample_kernel,
        grid=(B, nk),
        in_specs=[
            tok, cache, cache, tok, tok,
            pl.BlockSpec((None, H, T, 1), lambda b, j: (b, 0, 0, 0)),
            pl.BlockSpec((None, H, None, 1, tk), lambda b, j: (b, 0, j, 0, 0)),
            pl.BlockSpec((None, H, 1, T), lambda b, j: (b, 0, 0, 0)),
        ],
        out_specs=tok,
        out_shape=jax.ShapeDtypeStruct((B * T, WIDTH), BF16),
        scratch_shapes=[pltpu.VMEM((H, T, 1), F32), pltpu.VMEM((H, T, 1), F32), pltpu.VMEM((H, T, HEAD_DIM), F32)],
        compiler_params=_params(2),
        name="fox_sample",
    )(q, cache_k, cache_v, k_new, v_new, c_q, c_kc, c_kn)


def _retention_kernel(q_ref, k_ref, v_ref, g_ref, gn_ref, s0_ref, dm_ref, qd_ref, kd_ref, cd_ref,
                      o_ref, s_ref, *, L, hb, unroll):
    T = q_ref.shape[0]

    def chunk(off, h, s):
        rows = pl.ds(off, L)
        hs = slice(h * HEAD_DIM, (h + 1) * HEAD_DIM)
        q = q_ref[rows, hs]
        k = k_ref[rows, hs]
        v = v_ref[rows, hs].astype(BF16)
        att = _dot_nt(q, k) * dm_ref[h]
        o = _dot(att.astype(BF16), v) + _dot((q.astype(F32) * qd_ref[h]).astype(BF16), s.astype(BF16))
        kd = k.astype(F32) * kd_ref[h]
        s_new = s * cd_ref[h] + _dot(kd.T.astype(BF16), v)
        mu = jnp.mean(o, axis=-1, keepdims=True)
        oc = o - mu
        y = oc * lax.rsqrt(jnp.mean(oc * oc, axis=-1, keepdims=True) + EPS)
        o_ref[rows, hs] = (y * gn_ref[:, hs] * _silu(g_ref[rows, hs])).astype(o_ref.dtype)
        return s_new

    def body(it, states):
        states = list(states)
        for u in range(unroll):
            off = pl.multiple_of((it * unroll + u) * L, L)
            for h in range(hb):
                states[h] = chunk(off, h, states[h])
        return tuple(states)

    states = lax.fori_loop(0, T // (L * unroll), body, tuple(s0_ref[h] for h in range(hb)))
    for h in range(hb):
        s_ref[h] = states[h]


def _retention_consts(L):
    H = N_HEADS
    log_g = jnp.log1p(-jnp.exp2(-5.0 - jnp.arange(H, dtype=F32)))
    i = jnp.arange(L, dtype=F32)
    diff = i[:, None] - i[None, :]
    dmat = jnp.where(diff[None] >= 0, jnp.exp(jnp.maximum(diff, 0.0)[None] * log_g[:, None, None]), 0.0)
    q_dec = jnp.exp((i + 1.0)[None, :] * log_g[:, None])
    k_dec = jnp.exp((L - 1.0 - i)[None, :] * log_g[:, None])
    c_dec = jnp.exp(L * log_g)
    rep = lambda a: jnp.broadcast_to(a[..., None], a.shape + (LANES,))
    return dmat, rep(q_dec), rep(k_dec), rep(c_dec[:, None])


def _retention(qk, vg, gn, s0, B, T):
    H = N_HEADS
    L = min(RET_CHUNK, T)
    n_chunks = T // L
    unroll = math.gcd(RET_UNROLL, n_chunks)
    hb = 1 if n_chunks > 1 else H
    nh = H // hb
    dmat, q_dec, k_dec, c_dec = _retention_consts(L)
    seq = lambda off: pl.BlockSpec((T, hb * HEAD_DIM), lambda b, h: (b, h + off))
    per_head = lambda r: pl.BlockSpec((hb, r, LANES), lambda b, h: (h, 0, 0))
    state = pl.BlockSpec((None, hb, HEAD_DIM, HEAD_DIM), lambda b, h: (b, h, 0, 0))
    return pl.pallas_call(
        functools.partial(_retention_kernel, L=L, hb=hb, unroll=unroll),
        grid=(B, nh),
        in_specs=[seq(0), seq(nh), seq(0), seq(nh),
                  pl.BlockSpec((1, hb * HEAD_DIM), lambda b, h: (0, h)), state,
                  pl.BlockSpec((hb, L, L), lambda b, h: (h, 0, 0)), per_head(L), per_head(L), per_head(1)],
        out_specs=[seq(0), state],
        out_shape=[jax.ShapeDtypeStruct((B * T, WIDTH), BF16), jax.ShapeDtypeStruct((B, H, HEAD_DIM, HEAD_DIM), F32)],
        compiler_params=_params(2),
        name="retention",
    )(qk, qk, vg, vg, gn.reshape(1, WIDTH), s0, dmat, q_dec, k_dec, c_dec)


def _conv_kernel(x_ref, prev_ref, init_ref, w_ref, b_ref, g_ref, beta_ref, o_ref, xp_ref, xs_ref, z_ref, *, tr):
    i = pl.program_id(1)

    @pl.when(i == 0)
    def _():
        xp_ref[0:HALO, :] = init_ref[...]

    @pl.when(i > 0)
    def _():
        xp_ref[0:HALO, :] = prev_ref[...]

    C = x_ref.shape[1]
    xp_ref[HALO:HALO + tr, :] = x_ref[...]
    xp_ref[HALO + tr:HALO + tr + SUBLANES, :] = jnp.zeros((SUBLANES, C), F32)
    first = HALO - (CONV_WIDTH - 1)
    span = tr + CONV_SPAN
    for r in range(SUBLANES):
        xs_ref[r] = xp_ref[pl.ds(first + r, span), :]

    groups = CONV_SPAN // SUBLANES + 1
    for c in range(C // LANES):
        cs = slice(c * LANES, (c + 1) * LANES)
        wts = [w_ref[w, :, cs] for w in range(CONV_WIDTH)]
        bias = jnp.broadcast_to(b_ref[:, cs], (SUBLANES, LANES))

        def step(u, accs, cs=cs, wts=wts, bias=bias):
            accs = (bias,) + accs
            r0 = pl.multiple_of(u * SUBLANES, SUBLANES)
            for r in range(SUBLANES):
                x = xs_ref[r, pl.ds(r0, SUBLANES), cs]
                accs = tuple(acc + x * wts[SUBLANES * a + r] if SUBLANES * a + r < CONV_WIDTH else acc
                             for a, acc in enumerate(accs))
            z_ref[pl.ds(r0, SUBLANES), cs] = accs[-1]
            return accs[:-1]

        lax.fori_loop(0, span // SUBLANES, step, (bias,) * (groups - 1), unroll=CONV_UNROLL)

    z = z_ref[CONV_SPAN:CONV_SPAN + tr, :]
    mu = jnp.mean(z, axis=-1, keepdims=True)
    zc = z - mu
    y = zc * lax.rsqrt(jnp.mean(zc * zc, axis=-1, keepdims=True) + EPS) * g_ref[...] + beta_ref[...]
    o_ref[...] = _silu(y).astype(o_ref.dtype)


def _conv_module(glu, init, w_dw, b_dw, ln_g, ln_b, B, T, tr):
    C = glu.shape[1]
    nt = T // tr
    per = tr // HALO
    w_rep = jnp.broadcast_to(jnp.pad(w_dw, ((0, HALO - CONV_WIDTH), (0, 0)))[:, None, :], (HALO, SUBLANES, C))
    vec = pl.BlockSpec((1, C), lambda b, i: (0, 0))
    return pl.pallas_call(
        functools.partial(_conv_kernel, tr=tr),
        grid=(B, nt),
        in_specs=[
            pl.BlockSpec((tr, C), lambda b, i: (b * nt + i, 0)),
            pl.BlockSpec((HALO, C), lambda b, i: (b * nt * per + jnp.maximum(i * per - 1, 0), 0)),
            pl.BlockSpec((None, HALO, C), lambda b, i: (b, 0, 0)),
            pl.BlockSpec((HALO, SUBLANES, C), lambda b, i: (0, 0, 0)),
            vec, vec, vec,
        ],
        out_specs=pl.BlockSpec((tr, C), lambda b, i: (b * nt + i, 0)),
        out_shape=jax.ShapeDtypeStruct((B * T, C), BF16),
        scratch_shapes=[pltpu.VMEM((HALO + tr + SUBLANES, C), F32),
                        pltpu.VMEM((SUBLANES, tr + CONV_SPAN, C), F32),
                        pltpu.VMEM((tr + CONV_SPAN, C), F32)],
        compiler_params=_params(2),
        name="conv_module",
    )(glu, glu, init, w_rep, b_dw.reshape(1, C), ln_g.reshape(1, C), ln_b.reshape(1, C))


def _rope_tables(pos):
    half = HEAD_DIM // 2
    inv = jnp.exp(-math.log(ROPE_BASE) * jnp.arange(half, dtype=F32) / half)
    ang = pos.astype(F32)[:, None] * inv[None, :]
    cos, sin = jnp.cos(ang), jnp.sin(ang)
    return jnp.concatenate([cos, cos], axis=-1), jnp.concatenate([-sin, sin], axis=-1)


def _prepare_weights(norm_mix, norm_ffn, w_in, b_forget, q_norm_gain, k_norm_gain, ret_norm_gain, w_out,
                     w_pw1, b_pw1, w_dw, b_dw, conv_ln_gain, conv_ln_bias, w_pw2, b_pw2,
                     w_ffn_gate, w_ffn_up, w_ffn_down):
    W = WIDTH
    D = w_in.shape[1]
    f0 = 3 * W
    f1 = f0 + N_HEADS
    w_b = w_in[0, :, f1:]
    return dict(
        norm_mix=norm_mix, norm_ffn=norm_ffn,
        w_q=_Rhs(w_in, 0, D, col=0), w_k=_Rhs(w_in, 0, D, col=W), w_v=_Rhs(w_in, 0, D, col=2 * W),
        w_f=_Rhs(w_in, 0, D, col=f0),
        b_f=jnp.pad(b_forget[0], (0, LANES - N_HEADS)).reshape(1, LANES),
        w_qkb=_Rhs(w_b, None, D, col=0), w_vgb=_Rhs(w_b, None, D, col=2 * W),
        rope_scale=jnp.concatenate([jnp.ones((1, W), F32), jnp.full((1, W), HEAD_DIM ** -0.5, F32)], axis=1),
        gq=q_norm_gain[0].reshape(1, HEAD_DIM), gk=k_norm_gain[0].reshape(1, HEAD_DIM), gn=ret_norm_gain[0],
        w_oa=_Rhs(w_out, 0, W, k_blk=0), w_ob=_Rhs(w_out, 0, W, k_blk=1),
        w_1a=_Rhs(w_pw1, 0, D, col=0), w_1g=_Rhs(w_pw1, 0, D, col=w_pw1.shape[2] // 2),
        b_1a=b_pw1[0, :2 * W].reshape(1, -1), b_1g=b_pw1[0, 2 * W:].reshape(1, -1),
        w_dw=w_dw[0], b_dw=b_dw[0], ln_g=conv_ln_gain[0], ln_b=conv_ln_bias[0],
        w_2=_Rhs(w_pw2, 0, w_pw2.shape[1]), b_2=b_pw2[0].reshape(1, -1),
        w_g=[_Rhs(w_ffn_gate, l, D) for l in range(2)], w_u=[_Rhs(w_ffn_up, l, D) for l in range(2)],
        w_d=[_Rhs(w_ffn_down, l, w_ffn_down.shape[1]) for l in range(2)],
        d_ff=w_ffn_gate.shape[2],
    )


def _ffn(x, P, layer, tag):
    D = x.shape[1]
    h = _rmsnorm(x, P["norm_ffn"][layer], f"rms_ffn{layer}_{tag}")
    act, = _matmul(f"ffn_up{layer}_{tag}", [h], [P["w_g"][layer], P["w_u"][layer]], P["d_ff"],
                   [[(0, 0)], [(0, 1)]], _ep_swiglu, [BF16], tm=1024)
    out, = _matmul(f"ffn_down{layer}_{tag}", [act], [P["w_d"][layer]], D, [[(0, 0)]], _ep_residual, [F32],
                   tile=[x])
    return out


def _trunk(x3, pos, past, conv_past, P, tag):
    B, T, D = x3.shape
    M = B * T
    H = N_HEADS
    W = WIDTH
    x = x3.reshape(M, D)
    one = [[(0, 0)]]
    tall = dict(tm=1024)
    wide = dict(tm=1024, tn=1024)

    h = _rmsnorm(x, P["norm_mix"][0], f"rms_mix0_{tag}")
    q, = _matmul(f"proj_q_{tag}", [h], [P["w_q"]], W, one,
                 functools.partial(_ep_headnorm, scales=(HEAD_DIM ** -0.5 * LOG2E,)), [BF16], full=[P["gq"]], **wide)
    k32, k16 = _matmul(f"proj_k_{tag}", [h], [P["w_k"]], W, one,
                       functools.partial(_ep_headnorm, scales=(1.0, 1.0)), [F32, BF16], full=[P["gk"]], **wide)
    v32, v16 = _matmul(f"proj_v_{tag}", [h], [P["w_v"]], W, one, _ep_identity, [F32, BF16], **wide)
    logf_pad, = _matmul(f"proj_f_{tag}", [h], [P["w_f"]], LANES, one, _ep_logsigmoid, [F32], col=[P["b_f"]],
                        tn=LANES)
    cos, sin = _rope_tables(pos)
    reps = max(1, wide["tm"] // T)
    cos, sin = jnp.tile(cos, (reps, 1)), jnp.tile(sin, (reps, 1))
    qk_b, = _matmul(f"proj_qkb_{tag}", [h], [P["w_qkb"]], 2 * W, one, _ep_rope, [BF16], col=[P["rope_scale"]],
                    row=[cos, sin], **wide)
    vg_b, = _matmul(f"proj_vgb_{tag}", [h], [P["w_vgb"]], 2 * W, one, _ep_identity, [F32], **wide)

    logf = logf_pad[:, :H].reshape(B, T, H)
    lf_t = logf.transpose(0, 2, 1).reshape(B * H, T)
    if past is None:
        _, c2_pieces = _cumsum_rows(lf_t, f"cumsum_{tag}", LOG2E)
        oa = _fox_prompt(q, k16, v16, c2_pieces, B, T)
        s0 = jnp.zeros((B, H, HEAD_DIM, HEAD_DIM), F32)
    else:
        kc, vc, lfc, s0 = past
        Pl = kc.shape[2]
        lf_all = jnp.concatenate([lfc.transpose(0, 2, 1).reshape(B * H, Pl), lf_t], axis=1)
        pad = (-lf_all.shape[1]) % LANES
        c2, _ = _cumsum_rows(jnp.pad(lf_all, ((0, 0), (0, pad))), f"cumsum_{tag}", LOG2E)
        oa = _fox_sample(q, k16, v16, kc, vc, c2[:, :Pl + T], B, T)
    ob, s_new = _retention(qk_b, vg_b, P["gn"], s0, B, T)
    x, = _matmul(f"out_proj_{tag}", [oa, ob], [P["w_oa"], P["w_ob"]], D, [[(0, 0), (1, 1)]], _ep_residual, [F32],
                 tile=[x], **wide)
    x = _ffn(x, P, 0, tag)

    h = _rmsnorm(x, P["norm_mix"][1], f"rms_mix1_{tag}")
    glu, = _matmul(f"conv_pw1_{tag}", [h], [P["w_1a"], P["w_1g"]], D, [[(0, 0)], [(0, 1)]], _ep_glu, [F32],
                   col=[P["b_1a"], P["b_1g"]], **tall)
    keep = CONV_WIDTH - 1
    assert T >= keep
    if conv_past is None:
        init = jnp.zeros((B, HALO, D), F32)
    else:
        init = jnp.pad(conv_past, ((0, 0), (HALO - keep, 0), (0, 0)))
    z = _conv_module(glu, init, P["w_dw"], P["b_dw"], P["ln_g"], P["ln_b"], B, T, tr=min(256, T))
    conv_new = glu.reshape(B, T, D)[:, -keep:]
    x, = _matmul(f"conv_pw2_{tag}", [z], [P["w_2"]], D, one, _ep_bias_residual, [F32], tile=[x], col=[P["b_2"]],
                 **wide)
    x = _ffn(x, P, 1, tag)

    return (x.reshape(B, T, D), k32.reshape(1, B, T, H, HEAD_DIM), v32.reshape(1, B, T, H, HEAD_DIM),
            logf[None], s_new[None], conv_new[None])


def kernel(x_prompt, x_sample, cache_k, cache_v, cache_logf, state_ret, state_conv,
           norm_mix, norm_ffn, w_in, b_forget, q_norm_gain, k_norm_gain, ret_norm_gain, w_out,
           w_pw1, b_pw1, w_dw, b_dw, conv_ln_gain, conv_ln_bias, w_pw2, b_pw2,
           w_ffn_gate, w_ffn_up, w_ffn_down):
    P = _prepare_weights(norm_mix, norm_ffn, w_in, b_forget, q_norm_gain, k_norm_gain, ret_norm_gain, w_out,
                         w_pw1, b_pw1, w_dw, b_dw, conv_ln_gain, conv_ln_bias, w_pw2, b_pw2,
                         w_ffn_gate, w_ffn_up, w_ffn_down)
    pos_p = jnp.arange(x_prompt.shape[1])
    y_p, k_p, v_p, lf_p, r_p, c_p = _trunk(x_prompt, pos_p, None, None, P, "p")
    _, Bs, Pl, H, hd = cache_k.shape
    pos_s = Pl + jnp.arange(x_sample.shape[1])
    past = (cache_k, cache_v, cache_logf.reshape(Bs, Pl, H), state_ret.reshape(Bs, H, hd, hd))
    conv_past = state_conv.reshape(state_conv.shape[1:])
    y_s, k_s, v_s, lf_s, r_s, c_s = _trunk(x_sample, pos_s, past, conv_past, P, "s")
    return (y_p, y_s, k_p, v_p, lf_p, r_p, c_p, k_s, v_s, lf_s, r_s, c_s)
```

```python
import functools
import math
from typing import NamedTuple, Optional

import jax
import jax.numpy as jnp
from jax import lax
from jax.experimental import pallas as pl
from jax.experimental.pallas import tpu as pltpu

F32 = jnp.float32
BF16 = jnp.bfloat16

LANES = 128
SUBLANES = 8
HALO = 32
VMEM_LIMIT = 56 * 1024 * 1024

HEAD_DIM = 128
N_HEADS = 8
WIDTH = N_HEADS * HEAD_DIM
CONV_WIDTH = 31
ROPE_BASE = 10000.0
EPS = 1e-6
NEG = -1e30
LOG2E = math.log2(math.e)
CONV_SPAN = (CONV_WIDTH - 1) // SUBLANES * SUBLANES
CONV_UNROLL = 5
RET_CHUNK = 256
MM_ROW_SPLIT = 4
RET_UNROLL = 4


def _params(n_grid):
    return pltpu.CompilerParams(dimension_semantics=("arbitrary",) * n_grid, vmem_limit_bytes=VMEM_LIMIT)


def _silu(x):
    return x * jax.nn.sigmoid(x)


def _dot(a, b):
    return jnp.dot(a, b, preferred_element_type=F32)


def _dot_nt(a, b):
    return lax.dot_general(a, b, (((1,), (1,)), ((), ())), preferred_element_type=F32)


def _rms_kernel(x_ref, g_ref, o_ref):
    x = x_ref[...]
    ms = jnp.mean(x * x, axis=-1, keepdims=True)
    o_ref[...] = (x * lax.rsqrt(ms + EPS) * g_ref[...]).astype(o_ref.dtype)


def _rmsnorm(x, g, name, tm=512):
    M, D = x.shape
    return pl.pallas_call(
        _rms_kernel,
        grid=(M // tm,),
        in_specs=[pl.BlockSpec((tm, D), lambda i: (i, 0)), pl.BlockSpec((1, D), lambda i: (0, 0))],
        out_specs=pl.BlockSpec((tm, D), lambda i: (i, 0)),
        out_shape=jax.ShapeDtypeStruct((M, D), BF16),
        compiler_params=_params(1),
        name=name,
    )(x, g.reshape(1, D))


class _Rhs(NamedTuple):
    arr: jax.Array
    lead: Optional[int]
    k: int
    k_blk: int = 0
    col: int = 0


def _mm_kernel(*refs, n_lhs, n_rhs, n_extra, n_out, row_blocked, products, epilogue):
    lhs = refs[:n_lhs]
    rhs = refs[n_lhs:n_lhs + n_rhs]
    extras = refs[n_lhs + n_rhs:n_lhs + n_rhs + n_extra]
    outs = refs[n_lhs + n_rhs + n_extra:n_lhs + n_rhs + n_extra + n_out]
    wbuf = refs[n_lhs + n_rhs + n_extra + n_out:]

    @pl.when(pl.program_id(1) == 0)
    def _():
        for w, b in zip(rhs, wbuf):
            b[...] = w[...].astype(BF16)

    ts = lhs[0].shape[0] // MM_ROW_SPLIT

    def slab_dots(s):
        rows = pl.ds(s * ts, ts)
        accs = []
        for prod in products:
            acc = None
            for a, b in prod:
                d = _dot(lhs[a][rows, :], wbuf[b][...])
                acc = d if acc is None else acc + d
            accs.append(acc)
        return accs

    def slab_epilogue(s, accs):
        rows = pl.ds(s * ts, ts)
        epilogue(accs, [e.at[rows, :] if blocked else e for e, blocked in zip(extras, row_blocked)],
                 [o.at[rows, :] for o in outs])

    pending = slab_dots(0)
    for s in range(1, MM_ROW_SPLIT):
        nxt = slab_dots(s)
        slab_epilogue(s - 1, pending)
        pending = nxt
    slab_epilogue(MM_ROW_SPLIT - 1, pending)


def _matmul(name, lhs, rhs, n_cols, products, epilogue, out_dtypes, *, tile=(), col=(), row=(), full=(),
            tm=512, tn=512):
    M = lhs[0].shape[0]
    assert M % tm == 0 and n_cols % tn == 0
    in_specs = [pl.BlockSpec((tm, a.shape[1]), lambda j, i: (i, 0)) for a in lhs]
    for w in rhs:
        assert w.col % tn == 0
        c0 = w.col // tn
        if w.lead is None:
            in_specs.append(pl.BlockSpec((w.k, tn), lambda j, i, kb=w.k_blk, c0=c0: (kb, j + c0)))
        else:
            in_specs.append(pl.BlockSpec((None, w.k, tn),
                                         lambda j, i, ld=w.lead, kb=w.k_blk, c0=c0: (ld, kb, j + c0)))
    in_specs += [pl.BlockSpec((tm, tn), lambda j, i: (i, j)) for _ in tile]
    in_specs += [pl.BlockSpec((1, tn), lambda j, i: (0, j)) for _ in col]
    for r in row:
        assert r.shape[0] % tm == 0
        in_specs.append(pl.BlockSpec((tm, r.shape[1]), lambda j, i, nblk=r.shape[0] // tm: (i % nblk, 0)))
    in_specs += [pl.BlockSpec(f.shape, lambda j, i, nd=f.ndim: (0,) * nd) for f in full]
    extras = tuple(tile) + tuple(col) + tuple(row) + tuple(full)
    row_blocked = (True,) * len(tile) + (False,) * len(col) + (True,) * len(row) + (False,) * len(full)
    kern = functools.partial(_mm_kernel, n_lhs=len(lhs), n_rhs=len(rhs), n_extra=len(extras),
                             n_out=len(out_dtypes), row_blocked=row_blocked, products=products, epilogue=epilogue)
    return pl.pallas_call(
        kern,
        grid=(n_cols // tn, M // tm),
        in_specs=in_specs,
        out_specs=[pl.BlockSpec((tm, tn), lambda j, i: (i, j)) for _ in out_dtypes],
        out_shape=[jax.ShapeDtypeStruct((M, n_cols), dt) for dt in out_dtypes],
        scratch_shapes=[pltpu.VMEM((w.k, tn), BF16) for w in rhs],
        compiler_params=_params(2),
        name=name,
    )(*lhs, *[w.arr for w in rhs], *extras)


def _ep_headnorm(accs, extras, outs, *, scales):
    gain = extras[0][...]
    y = accs[0]
    for h in range(y.shape[1] // HEAD_DIM):
        sl = slice(h * HEAD_DIM, (h + 1) * HEAD_DIM)
        yh = y[:, sl]
        ms = jnp.mean(yh * yh, axis=-1, keepdims=True)
        r = yh * lax.rsqrt(ms + EPS) * gain
        for o, sc in zip(outs, scales):
            o[:, sl] = (r if sc == 1.0 else r * sc).astype(o.dtype)


def _ep_identity(accs, extras, outs):
    for o in outs:
        o[...] = accs[0].astype(o.dtype)


def _ep_logsigmoid(accs, extras, outs):
    z = accs[0] + extras[0][...]
    outs[0][...] = -(jnp.maximum(-z, 0.0) + jnp.log1p(jnp.exp(-jnp.abs(z))))


def _ep_rope(accs, extras, outs):
    scale, cos, sin = extras[0][...], extras[1][...], extras[2][...]
    y = accs[0]
    for h in range(y.shape[1] // HEAD_DIM):
        sl = slice(h * HEAD_DIM, (h + 1) * HEAD_DIM)
        yh = y[:, sl]
        r = yh * cos + pltpu.roll(yh, HEAD_DIM // 2, 1) * sin
        outs[0][:, sl] = (r * scale[:, sl]).astype(outs[0].dtype)


def _ep_residual(accs, extras, outs):
    outs[0][...] = extras[0][...] + accs[0]


def _ep_bias_residual(accs, extras, outs):
    outs[0][...] = extras[0][...] + (accs[0] + extras[1][...])


def _ep_swiglu(accs, extras, outs):
    outs[0][...] = (_silu(accs[0]) * accs[1]).astype(outs[0].dtype)


def _ep_glu(accs, extras, outs):
    a = accs[0] + extras[0][...]
    g = accs[1] + extras[1][...]
    outs[0][...] = a * jax.nn.sigmoid(g)


def _split3(x):
    hi = x.astype(BF16)
    r1 = x - hi.astype(F32)
    mid = r1.astype(BF16)
    lo = (r1 - mid.astype(F32)).astype(BF16)
    return hi, mid, lo


def _cumsum_kernel(x_ref, c_ref, hi_ref, mid_ref, lo_ref, *, scale):
    nblk, R, _ = x_ref.shape
    row = lax.broadcasted_iota(jnp.int32, (LANES, LANES), 0)
    col = lax.broadcasted_iota(jnp.int32, (LANES, LANES), 1)
    tri = jnp.where(row <= col, 1.0, 0.0).astype(BF16)
    ones = jnp.ones((LANES, LANES), BF16)

    def body(b, carry):
        hi, mid, lo = _split3(x_ref[b])
        within = _dot(hi, tri) + _dot(mid, tri) + _dot(lo, tri)
        total = _dot(hi, ones) + _dot(mid, ones) + _dot(lo, ones)
        c = (carry + within) * scale
        c_ref[b] = c
        hi_ref[b], mid_ref[b], lo_ref[b] = _split3(c)
        return carry + total

    lax.fori_loop(0, nblk, body, jnp.zeros((R, LANES), F32))


def _cumsum_rows(x, name, scale):
    R, N = x.shape
    nblk = N // LANES
    x3 = x.reshape(R, nblk, LANES).transpose(1, 0, 2)
    outs = pl.pallas_call(
        functools.partial(_cumsum_kernel, scale=scale),
        out_shape=[jax.ShapeDtypeStruct((nblk, R, LANES), dt) for dt in (F32, BF16, BF16, BF16)],
        compiler_params=pltpu.CompilerParams(vmem_limit_bytes=VMEM_LIMIT),
        name=name,
    )(x3)
    c, hi, mid, lo = [o.transpose(1, 0, 2).reshape(R, N) for o in outs]
    return c, (hi, mid, lo)


def _flash_step_t(st, vt, carry, mask):
    m, l, acc = carry
    if mask is not None:
        st = jnp.where(mask, st, NEG)
    m_new = jnp.maximum(m, jnp.max(st, axis=0, keepdims=True))
    alpha = jnp.exp2(m - m_new)
    pt = jnp.exp2(st - m_new)
    l = alpha * l + jnp.sum(pt, axis=0, keepdims=True)
    acc = alpha * acc + _dot(vt, pt.astype(BF16))
    return m_new, l, acc


def _flash_init_t(tq):
    return (jnp.full((1, tq), NEG, F32), jnp.zeros((1, tq), F32), jnp.zeros((HEAD_DIM, tq), F32))


def _causal_mask_t(tk, tq, q_off):
    keys = lax.broadcasted_iota(jnp.int32, (tk, tq), 0)
    queries = lax.broadcasted_iota(jnp.int32, (tk, tq), 1) + q_off
    return keys <= queries


def _fox_prompt_kernel(q_ref, qb_ref, k_ref, kb_ref, vt_ref, o_ref, *, tq, hb, qsplit):
    i = pl.program_id(2)
    tqc = tq // qsplit
    heads = [slice(h * HEAD_DIM, (h + 1) * HEAD_DIM) for h in range(hb)]
    chains = [(h, slice(c * tqc, (c + 1) * tqc)) for h in range(hb) for c in range(qsplit)]
    qs = [jnp.concatenate([q_ref[qr, heads[h]], qb_ref[h, qr, :]], axis=1) for h, qr in chains]

    def block(j, width, carries, diagonal):
        rows = pl.ds(pl.multiple_of(j * tq, tq), width * tq)
        kb = kb_ref[rows, :]
        ks = [jnp.concatenate([k_ref[rows, hs], kb], axis=1) for hs in heads]
        sts = [_dot_nt(ks[h], qs[n]) for n, (h, _) in enumerate(chains)]
        vts = [jnp.concatenate([vt_ref[h, j + w] for w in range(width)], axis=1) for h in range(hb)]
        return tuple(
            _flash_step_t(sts[n], vts[h], carries[n], _causal_mask_t(tq, tqc, qr.start) if diagonal else None)
            for n, (h, qr) in enumerate(chains))

    carries = lax.fori_loop(0, i // 2, lambda j, c: block(2 * j, 2, c, False), (_flash_init_t(tqc),) * len(chains))
    carries = lax.cond(i % 2 == 1, lambda c: block(i - 1, 1, c, False), lambda c: c, carries)
    carries = block(i, 1, carries, True)
    for (m, l, acc), (h, qr) in zip(carries, chains):
        o_ref[qr, heads[h]] = (acc / l).T.astype(o_ref.dtype)


def _bias_columns(pieces, hb):
    R, S = pieces[0].shape
    parts = jnp.stack(pieces, axis=-1)
    ones = jnp.ones_like(parts)
    q6 = jnp.concatenate([parts, ones], axis=-1)
    k6 = jnp.concatenate([ones, -parts], axis=-1)
    slot = jax.nn.one_hot(jnp.arange(R) % hb, hb, dtype=q6.dtype)
    qb = (q6[:, :, None, :] * slot[:, None, :, None]).reshape(R, S, 6 * hb)
    kb = k6.reshape(R // hb, hb, S, 6).transpose(0, 2, 1, 3).reshape(R // hb, S, 6 * hb)
    pad = ((0, 0), (0, 0), (0, LANES - 6 * hb))
    return jnp.pad(qb, pad), jnp.pad(kb, pad)


def _fox_prompt(q, k, v, c2_pieces, B, S, tq=512, hb=4, qsplit=1):
    H = N_HEADS
    nq = S // tq
    nh = H // hb
    qb, kb = _bias_columns(c2_pieces, hb)
    vt = v.reshape(B, nq, tq, H, HEAD_DIM).transpose(0, 3, 1, 4, 2).reshape(B * H, nq, HEAD_DIM, tq)
    return pl.pallas_call(
        functools.partial(_fox_prompt_kernel, tq=tq, hb=hb, qsplit=qsplit),
        grid=(B, nh, nq),
        in_specs=[
            pl.BlockSpec((tq, hb * HEAD_DIM), lambda b, h, i: (b * nq + i, h)),
            pl.BlockSpec((hb, tq, LANES), lambda b, h, i: (b * nh + h, i, 0)),
            pl.BlockSpec((S, hb * HEAD_DIM), lambda b, h, i: (b, h)),
            pl.BlockSpec((None, S, LANES), lambda b, h, i: (b * nh + h, 0, 0)),
            pl.BlockSpec((hb, nq, HEAD_DIM, tq), lambda b, h, i: (b * nh + h, 0, 0, 0)),
        ],
        out_specs=pl.BlockSpec((tq, hb * HEAD_DIM), lambda b, h, i: (b * nq + i, h)),
        out_shape=jax.ShapeDtypeStruct((B * S, WIDTH), BF16),
        compiler_params=_params(3),
        name="fox_prompt",
    )(q, qb, k, kb, vt)


def _causal_mask(tq, tk):
    rows = lax.broadcasted_iota(jnp.int32, (tq, tk), 0)
    cols = lax.broadcasted_iota(jnp.int32, (tq, tk), 1)
    return cols <= rows


def _softmax_update(s, m, l):
    m_new = jnp.maximum(m, jnp.max(s, axis=-1, keepdims=True))
    alpha = jnp.exp2(m - m_new)
    p = jnp.exp2(s - m_new)
    return m_new, alpha * l + jnp.sum(p, axis=-1, keepdims=True), alpha, p


def _fox_sample_kernel(q_ref, kc_ref, vc_ref, kn_ref, vn_ref, cq_ref, ckc_ref, ckn_ref, o_ref, m_scr, l_scr, acc_scr):
    j = pl.program_id(1)
    T = q_ref.shape[0]
    heads = [slice(h * HEAD_DIM, (h + 1) * HEAD_DIM) for h in range(N_HEADS)]

    @pl.when(j == 0)
    def _():
        m_scr[...] = jnp.full(m_scr.shape, NEG, F32)
        l_scr[...] = jnp.zeros(l_scr.shape, F32)
        acc_scr[...] = jnp.zeros(acc_scr.shape, F32)

    k_all = pltpu.einshape("mhd->hmd", kc_ref[...])
    v_all = pltpu.einshape("mhd->hmd", vc_ref[...])
    scores = [_dot_nt(q_ref[:, hs], k_all[h].astype(BF16)) + (cq_ref[h] - ckc_ref[h]) for h, hs in enumerate(heads)]
    stats = [_softmax_update(scores[h], m_scr[h], l_scr[h]) for h in range(N_HEADS)]
    for h, (m, l, alpha, p) in enumerate(stats):
        acc_scr[h] = alpha * acc_scr[h] + _dot(p.astype(BF16), v_all[h].astype(BF16))
        m_scr[h], l_scr[h] = m, l

    @pl.when(j == pl.num_programs(1) - 1)
    def _():
        mask = _causal_mask(T, T)
        for h, hs in enumerate(heads):
            s = jnp.where(mask, _dot_nt(q_ref[:, hs], kn_ref[:, hs]) + (cq_ref[h] - ckn_ref[h]), NEG)
            _, l, alpha, p = _softmax_update(s, m_scr[h], l_scr[h])
            acc = alpha * acc_scr[h] + _dot(p.astype(BF16), vn_ref[:, hs])
            o_ref[:, hs] = (acc / l).astype(o_ref.dtype)


def _fox_sample(q, k_new, v_new, cache_k, cache_v, c2, B, T, tk=1024):
    H = N_HEADS
    P = cache_k.shape[2]
    nk = P // tk
    c_q = c2[:, P:].reshape(B, H, T, 1)
    c_kc = c2[:, :P].reshape(B, H, nk, 1, tk)
    c_kn = c2[:, P:].reshape(B, H, 1, T)
    tok = pl.BlockSpec((T, WIDTH), lambda b, j: (b, 0))
    cache = pl.BlockSpec((None, None, tk, H, HEAD_DIM), lambda b, j: (0, b, j, 0, 0))
    return pl.pallas_call(
        _fox_sample_kernel,
        grid=(B, nk),
        in_specs=[
            tok, cache, cache, tok, tok,
            pl.BlockSpec((None, H, T, 1), lambda b, j: (b, 0, 0, 0)),
            pl.BlockSpec((None, H, None, 1, tk), lambda b, j: (b, 0, j, 0, 0)),
            pl.BlockSpec((None, H, 1, T), lambda b, j: (b, 0, 0, 0)),
        ],
        out_specs=tok,
        out_shape=jax.ShapeDtypeStruct((B * T, WIDTH), BF16),
        scratch_shapes=[pltpu.VMEM((H, T, 1), F32), pltpu.VMEM((H, T, 1), F32), pltpu.VMEM((H, T, HEAD_DIM), F32)],
        compiler_params=_params(2),
        name="fox_sample",
    )(q, cache_k, cache_v, k_new, v_new, c_q, c_kc, c_kn)


def _retention_kernel(q_ref, k_ref, v_ref, g_ref, gn_ref, s0_ref, dm_ref, qd_ref, kd_ref, cd_ref,
                      o_ref, s_ref, *, L, hb, unroll):
    T = q_ref.shape[0]

    def chunk(off, h, s):
        rows = pl.ds(off, L)
        hs = slice(h * HEAD_DIM, (h + 1) * HEAD_DIM)
        q = q_ref[rows, hs]
        k = k_ref[rows, hs]
        v = v_ref[rows, hs].astype(BF16)
        att = _dot_nt(q, k) * dm_ref[h]
        o = _dot(att.astype(BF16), v) + _dot((q.astype(F32) * qd_ref[h]).astype(BF16), s.astype(BF16))
        kd = k.astype(F32) * kd_ref[h]
        s_new = s * cd_ref[h] + _dot(kd.T.astype(BF16), v)
        mu = jnp.mean(o, axis=-1, keepdims=True)
        oc = o - mu
        y = oc * lax.rsqrt(jnp.mean(oc * oc, axis=-1, keepdims=True) + EPS)
        o_ref[rows, hs] = (y * gn_ref[:, hs] * _silu(g_ref[rows, hs])).astype(o_ref.dtype)
        return s_new

    def body(it, states):
        states = list(states)
        for u in range(unroll):
            off = pl.multiple_of((it * unroll + u) * L, L)
            for h in range(hb):
                states[h] = chunk(off, h, states[h])
        return tuple(states)

    states = lax.fori_loop(0, T // (L * unroll), body, tuple(s0_ref[h] for h in range(hb)))
    for h in range(hb):
        s_ref[h] = states[h]


def _retention_consts(L):
    H = N_HEADS
    log_g = jnp.log1p(-jnp.exp2(-5.0 - jnp.arange(H, dtype=F32)))
    i = jnp.arange(L, dtype=F32)
    diff = i[:, None] - i[None, :]
    dmat = jnp.where(diff[None] >= 0, jnp.exp(jnp.maximum(diff, 0.0)[None] * log_g[:, None, None]), 0.0)
    q_dec = jnp.exp((i + 1.0)[None, :] * log_g[:, None])
    k_dec = jnp.exp((L - 1.0 - i)[None, :] * log_g[:, None])
    c_dec = jnp.exp(L * log_g)
    rep = lambda a: jnp.broadcast_to(a[..., None], a.shape + (LANES,))
    return dmat, rep(q_dec), rep(k_dec), rep(c_dec[:, None])


def _retention(qk, vg, gn, s0, B, T):
    H = N_HEADS
    L = min(RET_CHUNK, T)
    n_chunks = T // L
    unroll = math.gcd(RET_UNROLL, n_chunks)
    hb = 1 if n_chunks > 1 else H
    nh = H // hb
    dmat, q_dec, k_dec, c_dec = _retention_consts(L)
    seq = lambda off: pl.BlockSpec((T, hb * HEAD_DIM), lambda b, h: (b, h + off))
    per_head = lambda r: pl.BlockSpec((hb, r, LANES), lambda b, h: (h, 0, 0))
    state = pl.BlockSpec((None, hb, HEAD_DIM, HEAD_DIM), lambda b, h: (b, h, 0, 0))
    return pl.pallas_call(
        functools.partial(_retention_kernel, L=L, hb=hb, unroll=unroll),
        grid=(B, nh),
        in_specs=[seq(0), seq(nh), seq(0), seq(nh),
                  pl.BlockSpec((1, hb * HEAD_DIM), lambda b, h: (0, h)), state,
                  pl.BlockSpec((hb, L, L), lambda b, h: (h, 0, 0)), per_head(L), per_head(L), per_head(1)],
        out_specs=[seq(0), state],
        out_shape=[jax.ShapeDtypeStruct((B * T, WIDTH), BF16), jax.ShapeDtypeStruct((B, H, HEAD_DIM, HEAD_DIM), F32)],
        compiler_params=_params(2),
        name="retention",
    )(qk, qk, vg, vg, gn.reshape(1, WIDTH), s0, dmat, q_dec, k_dec, c_dec)


def _conv_kernel(x_ref, prev_ref, init_ref, w_ref, b_ref, g_ref, beta_ref, o_ref, xp_ref, xs_ref, z_ref, *, tr):
    i = pl.program_id(1)

    @pl.when(i == 0)
    def _():
        xp_ref[0:HALO, :] = init_ref[...]

    @pl.when(i > 0)
    def _():
        xp_ref[0:HALO, :] = prev_ref[...]

    C = x_ref.shape[1]
    xp_ref[HALO:HALO + tr, :] = x_ref[...]
    xp_ref[HALO + tr:HALO + tr + SUBLANES, :] = jnp.zeros((SUBLANES, C), F32)
    first = HALO - (CONV_WIDTH - 1)
    span = tr + CONV_SPAN
    for r in range(SUBLANES):
        xs_ref[r] = xp_ref[pl.ds(first + r, span), :]

    groups = CONV_SPAN // SUBLANES + 1
    for c in range(C // LANES):
        cs = slice(c * LANES, (c + 1) * LANES)
        wts = [w_ref[w, :, cs] for w in range(CONV_WIDTH)]
        bias = jnp.broadcast_to(b_ref[:, cs], (SUBLANES, LANES))

        def step(u, accs, cs=cs, wts=wts, bias=bias):
            accs = (bias,) + accs
            r0 = pl.multiple_of(u * SUBLANES, SUBLANES)
            for r in range(SUBLANES):
                x = xs_ref[r, pl.ds(r0, SUBLANES), cs]
                accs = tuple(acc + x * wts[SUBLANES * a + r] if SUBLANES * a + r < CONV_WIDTH else acc
                             for a, acc in enumerate(accs))
            z_ref[pl.ds(r0, SUBLANES), cs] = accs[-1]
            return accs[:-1]

        lax.fori_loop(0, span // SUBLANES, step, (bias,) * (groups - 1), unroll=CONV_UNROLL)

    z = z_ref[CONV_SPAN:CONV_SPAN + tr, :]
    mu = jnp.mean(z, axis=-1, keepdims=True)
    zc = z - mu
    y = zc * lax.rsqrt(jnp.mean(zc * zc, axis=-1, keepdims=True) + EPS) * g_ref[...] + beta_ref[...]
    o_ref[...] = _silu(y).astype(o_ref.dtype)


def _conv_module(glu, init, w_dw, b_dw, ln_g, ln_b, B, T, tr):
    C = glu.shape[1]
    nt = T // tr
    per = tr // HALO
    w_rep = jnp.broadcast_to(jnp.pad(w_dw, ((0, HALO - CONV_WIDTH), (0, 0)))[:, None, :], (HALO, SUBLANES, C))
    vec = pl.BlockSpec((1, C), lambda b, i: (0, 0))
    return pl.pallas_call(
        functools.partial(_conv_kernel, tr=tr),
        grid=(B, nt),
        in_specs=[
            pl.BlockSpec((tr, C), lambda b, i: (b * nt + i, 0)),
            pl.BlockSpec((HALO, C), lambda b, i: (b * nt * per + jnp.maximum(i * per - 1, 0), 0)),
            pl.BlockSpec((None, HALO, C), lambda b, i: (b, 0, 0)),
            pl.BlockSpec((HALO, SUBLANES, C), lambda b, i: (0, 0, 0)),
            vec, vec, vec,
        ],
        out_specs=pl.BlockSpec((tr, C), lambda b, i: (b * nt + i, 0)),
        out_shape=jax.ShapeDtypeStruct((B * T, C), BF16),
        scratch_shapes=[pltpu.VMEM((HALO + tr + SUBLANES, C), F32),
                        pltpu.VMEM((SUBLANES, tr + CONV_SPAN, C), F32),
                        pltpu.VMEM((tr + CONV_SPAN, C), F32)],
        compiler_params=_params(2),
        name="conv_module",
    )(glu, glu, init, w_rep, b_dw.reshape(1, C), ln_g.reshape(1, C), ln_b.reshape(1, C))


def _rope_tables(pos):
    half = HEAD_DIM // 2
    inv = jnp.exp(-math.log(ROPE_BASE) * jnp.arange(half, dtype=F32) / half)
    ang = pos.astype(F32)[:, None] * inv[None, :]
    cos, sin = jnp.cos(ang), jnp.sin(ang)
    return jnp.concatenate([cos, cos], axis=-1), jnp.concatenate([-sin, sin], axis=-1)


def _prepare_weights(norm_mix, norm_ffn, w_in, b_forget, q_norm_gain, k_norm_gain, ret_norm_gain, w_out,
                     w_pw1, b_pw1, w_dw, b_dw, conv_ln_gain, conv_ln_bias, w_pw2, b_pw2,
                     w_ffn_gate, w_ffn_up, w_ffn_down):
    W = WIDTH
    D = w_in.shape[1]
    f0 = 3 * W
    f1 = f0 + N_HEADS
    w_b = w_in[0, :, f1:]
    return dict(
        norm_mix=norm_mix, norm_ffn=norm_ffn,
        w_q=_Rhs(w_in, 0, D, col=0), w_k=_Rhs(w_in, 0, D, col=W), w_v=_Rhs(w_in, 0, D, col=2 * W),
        w_f=_Rhs(w_in, 0, D, col=f0),
        b_f=jnp.pad(b_forget[0], (0, LANES - N_HEADS)).reshape(1, LANES),
        w_qkb=_Rhs(w_b, None, D, col=0), w_vgb=_Rhs(w_b, None, D, col=2 * W),
        rope_scale=jnp.concatenate([jnp.ones((1, W), F32), jnp.full((1, W), HEAD_DIM ** -0.5, F32)], axis=1),
        gq=q_norm_gain[0].reshape(1, HEAD_DIM), gk=k_norm_gain[0].reshape(1, HEAD_DIM), gn=ret_norm_gain[0],
        w_oa=_Rhs(w_out, 0, W, k_blk=0), w_ob=_Rhs(w_out, 0, W, k_blk=1),
        w_1a=_Rhs(w_pw1, 0, D, col=0), w_1g=_Rhs(w_pw1, 0, D, col=w_pw1.shape[2] // 2),
        b_1a=b_pw1[0, :2 * W].reshape(1, -1), b_1g=b_pw1[0, 2 * W:].reshape(1, -1),
        w_dw=w_dw[0], b_dw=b_dw[0], ln_g=conv_ln_gain[0], ln_b=conv_ln_bias[0],
        w_2=_Rhs(w_pw2, 0, w_pw2.shape[1]), b_2=b_pw2[0].reshape(1, -1),
        w_g=[_Rhs(w_ffn_gate, l, D) for l in range(2)], w_u=[_Rhs(w_ffn_up, l, D) for l in range(2)],
        w_d=[_Rhs(w_ffn_down, l, w_ffn_down.shape[1]) for l in range(2)],
        d_ff=w_ffn_gate.shape[2],
    )


def _ffn(x, P, layer, tag):
    D = x.shape[1]
    h = _rmsnorm(x, P["norm_ffn"][layer], f"rms_ffn{layer}_{tag}")
    act, = _matmul(f"ffn_up{layer}_{tag}", [h], [P["w_g"][layer], P["w_u"][layer]], P["d_ff"],
                   [[(0, 0)], [(0, 1)]], _ep_swiglu, [BF16], tm=1024)
    out, = _matmul(f"ffn_down{layer}_{tag}", [act], [P["w_d"][layer]], D, [[(0, 0)]], _ep_residual, [F32],
                   tile=[x])
    return out


def _trunk(x3, pos, past, conv_past, P, tag):
    B, T, D = x3.shape
    M = B * T
    H = N_HEADS
    W = WIDTH
    x = x3.reshape(M, D)
    one = [[(0, 0)]]
    tall = dict(tm=1024)
    wide = dict(tm=1024, tn=1024)

    h = _rmsnorm(x, P["norm_mix"][0], f"rms_mix0_{tag}")
    q, = _matmul(f"proj_q_{tag}", [h], [P["w_q"]], W, one,
                 functools.partial(_ep_headnorm, scales=(HEAD_DIM ** -0.5 * LOG2E,)), [BF16], full=[P["gq"]], **wide)
    k32, k16 = _matmul(f"proj_k_{tag}", [h], [P["w_k"]], W, one,
                       functools.partial(_ep_headnorm, scales=(1.0, 1.0)), [F32, BF16], full=[P["gk"]], **wide)
    v32, v16 = _matmul(f"proj_v_{tag}", [h], [P["w_v"]], W, one, _ep_identity, [F32, BF16], **wide)
    logf_pad, = _matmul(f"proj_f_{tag}", [h], [P["w_f"]], LANES, one, _ep_logsigmoid, [F32], col=[P["b_f"]],
                        tn=LANES)
    cos, sin = _rope_tables(pos)
    reps = max(1, wide["tm"] // T)
    cos, sin = jnp.tile(cos, (reps, 1)), jnp.tile(sin, (reps, 1))
    qk_b, = _matmul(f"proj_qkb_{tag}", [h], [P["w_qkb"]], 2 * W, one, _ep_rope, [BF16], col=[P["rope_scale"]],
                    row=[cos, sin], **wide)
    vg_b, = _matmul(f"proj_vgb_{tag}", [h], [P["w_vgb"]], 2 * W, one, _ep_identity, [F32], **wide)

    logf = logf_pad[:, :H].reshape(B, T, H)
    lf_t = logf.transpose(0, 2, 1).reshape(B * H, T)
    if past is None:
        _, c2_pieces = _cumsum_rows(lf_t, f"cumsum_{tag}", LOG2E)
        oa = _fox_prompt(q, k16, v16, c2_pieces, B, T)
        s0 = jnp.zeros((B, H, HEAD_DIM, HEAD_DIM), F32)
    else:
        kc, vc, lfc, s0 = past
        Pl = kc.shape[2]
        lf_all = jnp.concatenate([lfc.transpose(0, 2, 1).reshape(B * H, Pl), lf_t], axis=1)
        pad = (-lf_all.shape[1]) % LANES
        c2, _ = _cumsum_rows(jnp.pad(lf_all, ((0, 0), (0, pad))), f"cumsum_{tag}", LOG2E)
        oa = _fox_sample(q, k16, v16, kc, vc, c2[:, :Pl + T], B, T)
    ob, s_new = _retention(qk_b, vg_b, P["gn"], s0, B, T)
    x, = _matmul(f"out_proj_{tag}", [oa, ob], [P["w_oa"], P["w_ob"]], D, [[(0, 0), (1, 1)]], _ep_residual, [F32],
                 tile=[x], **wide)
    x = _ffn(x, P, 0, tag)

    h = _rmsnorm(x, P["norm_mix"][1], f"rms_mix1_{tag}")
    glu, = _matmul(f"conv_pw1_{tag}", [h], [P["w_1a"], P["w_1g"]], D, [[(0, 0)], [(0, 1)]], _ep_glu, [F32],
                   col=[P["b_1a"], P["b_1g"]], **tall)
    keep = CONV_WIDTH - 1
    assert T >= keep
    if conv_past is None:
        init = jnp.zeros((B, HALO, D), F32)
    else:
        init = jnp.pad(conv_past, ((0, 0), (HALO - keep, 0), (0, 0)))
    z = _conv_module(glu, init, P["w_dw"], P["b_dw"], P["ln_g"], P["ln_b"], B, T, tr=min(256, T))
    conv_new = glu.reshape(B, T, D)[:, -keep:]
    x, = _matmul(f"conv_pw2_{tag}", [z], [P["w_2"]], D, one, _ep_bias_residual, [F32], tile=[x], col=[P["b_2"]],
                 **wide)
    x = _ffn(x, P, 1, tag)

    return (x.reshape(B, T, D), k32.reshape(1, B, T, H, HEAD_DIM), v32.reshape(1, B, T, H, HEAD_DIM),
            logf[None], s_new[None], conv_new[None])


def kernel(x_prompt, x_sample, cache_k, cache_v, cache_logf, state_ret, state_conv,
           norm_mix, norm_ffn, w_in, b_forget, q_norm_gain, k_norm_gain, ret_norm_gain, w_out,
           w_pw1, b_pw1, w_dw, b_dw, conv_ln_gain, conv_ln_bias, w_pw2, b_pw2,
           w_ffn_gate, w_ffn_up, w_ffn_down):
    P = _prepare_weights(norm_mix, norm_ffn, w_in, b_forget, q_norm_gain, k_norm_gain, ret_norm_gain, w_out,
                         w_pw1, b_pw1, w_dw, b_dw, conv_ln_gain, conv_ln_bias, w_pw2, b_pw2,
                         w_ffn_gate, w_ffn_up, w_ffn_down)
    pos_p = jnp.arange(x_prompt.shape[1])
    y_p, k_p, v_p, lf_p, r_p, c_p = _trunk(x_prompt, pos_p, None, None, P, "p")
    _, Bs, Pl, H, hd = cache_k.shape
    pos_s = Pl + jnp.arange(x_sample.shape[1])
    past = (cache_k, cache_v, cache_logf.reshape(Bs, Pl, H), state_ret.reshape(Bs, H, hd, hd))
    conv_past = state_conv.reshape(state_conv.shape[1:])
    y_s, k_s, v_s, lf_s, r_s, c_s = _trunk(x_sample, pos_s, past, conv_past, P, "s")
    return (y_p, y_s, k_p, v_p, lf_p, r_p, c_p, k_s, v_s, lf_s, r_s, c_s)
```

```python
import functools
import math
from typing import NamedTuple, Optional

import jax
import jax.numpy as jnp
from jax import lax
from jax.experimental import pallas as pl
from jax.experimental.pallas import tpu as pltpu

F32 = jnp.float32
BF16 = jnp.bfloat16

LANES = 128
SUBLANES = 8
HALO = 32
VMEM_LIMIT = 56 * 1024 * 1024

HEAD_DIM = 128
N_HEADS = 8
WIDTH = N_HEADS * HEAD_DIM
CONV_WIDTH = 31
ROPE_BASE = 10000.0
EPS = 1e-6
NEG = -1e30
LOG2E = math.log2(math.e)
CONV_SPAN = (CONV_WIDTH - 1) // SUBLANES * SUBLANES
CONV_UNROLL = 5
RET_CHUNK = 256
EPILOGUE_SLABS = 4
RET_UNROLL = 4


def _params(n_grid):
    return pltpu.CompilerParams(dimension_semantics=("arbitrary",) * n_grid, vmem_limit_bytes=VMEM_LIMIT)


def _silu(x):
    return x * jax.nn.sigmoid(x)


def _dot(a, b):
    return jnp.dot(a, b, preferred_element_type=F32)


def _dot_nt(a, b):
    return lax.dot_general(a, b, (((1,), (1,)), ((), ())), preferred_element_type=F32)


def _rms_kernel(x_ref, g_ref, o_ref):
    x = x_ref[...]
    ms = jnp.mean(x * x, axis=-1, keepdims=True)
    o_ref[...] = (x * lax.rsqrt(ms + EPS) * g_ref[...]).astype(o_ref.dtype)


def _rmsnorm(x, g, name, tm=512):
    M, D = x.shape
    return pl.pallas_call(
        _rms_kernel,
        grid=(M // tm,),
        in_specs=[pl.BlockSpec((tm, D), lambda i: (i, 0)), pl.BlockSpec((1, D), lambda i: (0, 0))],
        out_specs=pl.BlockSpec((tm, D), lambda i: (i, 0)),
        out_shape=jax.ShapeDtypeStruct((M, D), BF16),
        compiler_params=_params(1),
        name=name,
    )(x, g.reshape(1, D))


class _Rhs(NamedTuple):
    arr: jax.Array
    lead: Optional[int]
    k: int
    k_blk: int = 0
    col: int = 0


def _mm_kernel(*refs, n_lhs, n_rhs, n_extra, n_out, row_blocked, slabs, products, epilogue):
    lhs = refs[:n_lhs]
    rhs = refs[n_lhs:n_lhs + n_rhs]
    extras = refs[n_lhs + n_rhs:n_lhs + n_rhs + n_extra]
    outs = refs[n_lhs + n_rhs + n_extra:n_lhs + n_rhs + n_extra + n_out]
    wbuf = refs[n_lhs + n_rhs + n_extra + n_out:]

    @pl.when(pl.program_id(1) == 0)
    def _():
        for w, b in zip(rhs, wbuf):
            b[...] = w[...].astype(BF16)

    ts = lhs[0].shape[0] // slabs

    def slab_dots(s):
        rows = pl.ds(s * ts, ts)
        accs = []
        for prod in products:
            acc = None
            for a, b in prod:
                d = _dot(lhs[a][rows, :], wbuf[b][...])
                acc = d if acc is None else acc + d
            accs.append(acc)
        return accs

    def slab_epilogue(s, accs):
        rows = pl.ds(s * ts, ts)
        epilogue(accs, [e.at[rows, :] if blocked else e for e, blocked in zip(extras, row_blocked)],
                 [o.at[rows, :] for o in outs])

    pending = slab_dots(0)
    for s in range(1, slabs):
        nxt = slab_dots(s)
        slab_epilogue(s - 1, pending)
        pending = nxt
    slab_epilogue(slabs - 1, pending)


def _matmul(name, lhs, rhs, n_cols, products, epilogue, out_dtypes, *, tile=(), col=(), row=(), full=(),
            tm=512, tn=512, slabs=1):
    M = lhs[0].shape[0]
    assert M % tm == 0 and n_cols % tn == 0
    in_specs = [pl.BlockSpec((tm, a.shape[1]), lambda j, i: (i, 0)) for a in lhs]
    for w in rhs:
        assert w.col % tn == 0
        c0 = w.col // tn
        if w.lead is None:
            in_specs.append(pl.BlockSpec((w.k, tn), lambda j, i, kb=w.k_blk, c0=c0: (kb, j + c0)))
        else:
            in_specs.append(pl.BlockSpec((None, w.k, tn),
                                         lambda j, i, ld=w.lead, kb=w.k_blk, c0=c0: (ld, kb, j + c0)))
    in_specs += [pl.BlockSpec((tm, tn), lambda j, i: (i, j)) for _ in tile]
    in_specs += [pl.BlockSpec((1, tn), lambda j, i: (0, j)) for _ in col]
    for r in row:
        assert r.shape[0] % tm == 0
        in_specs.append(pl.BlockSpec((tm, r.shape[1]), lambda j, i, nblk=r.shape[0] // tm: (i % nblk, 0)))
    in_specs += [pl.BlockSpec(f.shape, lambda j, i, nd=f.ndim: (0,) * nd) for f in full]
    extras = tuple(tile) + tuple(col) + tuple(row) + tuple(full)
    row_blocked = (True,) * len(tile) + (False,) * len(col) + (True,) * len(row) + (False,) * len(full)
    kern = functools.partial(_mm_kernel, n_lhs=len(lhs), n_rhs=len(rhs), n_extra=len(extras),
                             n_out=len(out_dtypes), row_blocked=row_blocked, slabs=slabs, products=products,
                             epilogue=epilogue)
    return pl.pallas_call(
        kern,
        grid=(n_cols // tn, M // tm),
        in_specs=in_specs,
        out_specs=[pl.BlockSpec((tm, tn), lambda j, i: (i, j)) for _ in out_dtypes],
        out_shape=[jax.ShapeDtypeStruct((M, n_cols), dt) for dt in out_dtypes],
        scratch_shapes=[pltpu.VMEM((w.k, tn), BF16) for w in rhs],
        compiler_params=_params(2),
        name=name,
    )(*lhs, *[w.arr for w in rhs], *extras)


def _ep_headnorm(accs, extras, outs, *, scales):
    gain = extras[0][...]
    y = accs[0]
    for h in range(y.shape[1] // HEAD_DIM):
        sl = slice(h * HEAD_DIM, (h + 1) * HEAD_DIM)
        yh = y[:, sl]
        ms = jnp.mean(yh * yh, axis=-1, keepdims=True)
        r = yh * lax.rsqrt(ms + EPS) * gain
        for o, sc in zip(outs, scales):
            o[:, sl] = (r if sc == 1.0 else r * sc).astype(o.dtype)


def _ep_identity(accs, extras, outs):
    for o in outs:
        o[...] = accs[0].astype(o.dtype)


def _ep_logsigmoid(accs, extras, outs):
    z = accs[0] + extras[0][...]
    outs[0][...] = -(jnp.maximum(-z, 0.0) + jnp.log1p(jnp.exp(-jnp.abs(z))))


def _ep_rope(accs, extras, outs):
    scale, cos, sin = extras[0][...], extras[1][...], extras[2][...]
    y = accs[0]
    for h in range(y.shape[1] // HEAD_DIM):
        sl = slice(h * HEAD_DIM, (h + 1) * HEAD_DIM)
        yh = y[:, sl]
        r = yh * cos + pltpu.roll(yh, HEAD_DIM // 2, 1) * sin
        outs[0][:, sl] = (r * scale[:, sl]).astype(outs[0].dtype)


def _ep_residual(accs, extras, outs):
    outs[0][...] = extras[0][...] + accs[0]


def _ep_bias_residual(accs, extras, outs):
    outs[0][...] = extras[0][...] + (accs[0] + extras[1][...])


def _ep_swiglu(accs, extras, outs):
    outs[0][...] = (_silu(accs[0]) * accs[1]).astype(outs[0].dtype)


def _ep_glu(accs, extras, outs):
    a = accs[0] + extras[0][...]
    g = accs[1] + extras[1][...]
    outs[0][...] = a * jax.nn.sigmoid(g)


def _split3(x):
    hi = x.astype(BF16)
    r1 = x - hi.astype(F32)
    mid = r1.astype(BF16)
    lo = (r1 - mid.astype(F32)).astype(BF16)
    return hi, mid, lo


def _cumsum_kernel(x_ref, c_ref, hi_ref, mid_ref, lo_ref, *, scale):
    nblk, R, _ = x_ref.shape
    row = lax.broadcasted_iota(jnp.int32, (LANES, LANES), 0)
    col = lax.broadcasted_iota(jnp.int32, (LANES, LANES), 1)
    tri = jnp.where(row <= col, 1.0, 0.0).astype(BF16)
    ones = jnp.ones((LANES, LANES), BF16)

    def body(b, carry):
        hi, mid, lo = _split3(x_ref[b])
        within = _dot(hi, tri) + _dot(mid, tri) + _dot(lo, tri)
        total = _dot(hi, ones) + _dot(mid, ones) + _dot(lo, ones)
        c = (carry + within) * scale
        c_ref[b] = c
        hi_ref[b], mid_ref[b], lo_ref[b] = _split3(c)
        return carry + total

    lax.fori_loop(0, nblk, body, jnp.zeros((R, LANES), F32))


def _cumsum_rows(x, name, scale):
    R, N = x.shape
    nblk = N // LANES
    x3 = x.reshape(R, nblk, LANES).transpose(1, 0, 2)
    outs = pl.pallas_call(
        functools.partial(_cumsum_kernel, scale=scale),
        out_shape=[jax.ShapeDtypeStruct((nblk, R, LANES), dt) for dt in (F32, BF16, BF16, BF16)],
        compiler_params=pltpu.CompilerParams(vmem_limit_bytes=VMEM_LIMIT),
        name=name,
    )(x3)
    c, hi, mid, lo = [o.transpose(1, 0, 2).reshape(R, N) for o in outs]
    return c, (hi, mid, lo)


def _flash_step_t(st, vt, carry, mask):
    m, l, acc = carry
    if mask is not None:
        st = jnp.where(mask, st, NEG)
    m_new = jnp.maximum(m, jnp.max(st, axis=0, keepdims=True))
    alpha = jnp.exp2(m - m_new)
    pt = jnp.exp2(st - m_new)
    l = alpha * l + jnp.sum(pt, axis=0, keepdims=True)
    acc = alpha * acc + _dot(vt, pt.astype(BF16))
    return m_new, l, acc


def _flash_init_t(tq):
    return (jnp.full((1, tq), NEG, F32), jnp.zeros((1, tq), F32), jnp.zeros((HEAD_DIM, tq), F32))


def _causal_mask_t(tk, tq, q_off):
    keys = lax.broadcasted_iota(jnp.int32, (tk, tq), 0)
    queries = lax.broadcasted_iota(jnp.int32, (tk, tq), 1) + q_off
    return keys <= queries


def _fox_prompt_kernel(q_ref, qb_ref, k_ref, kb_ref, vt_ref, o_ref, *, tq, hb, qsplit):
    i = pl.program_id(2)
    tqc = tq // qsplit
    heads = [slice(h * HEAD_DIM, (h + 1) * HEAD_DIM) for h in range(hb)]
    chains = [(h, slice(c * tqc, (c + 1) * tqc)) for h in range(hb) for c in range(qsplit)]
    qs = [jnp.concatenate([q_ref[qr, heads[h]], qb_ref[h, qr, :]], axis=1) for h, qr in chains]

    def block(j, width, carries, diagonal):
        rows = pl.ds(pl.multiple_of(j * tq, tq), width * tq)
        kb = kb_ref[rows, :]
        ks = [jnp.concatenate([k_ref[rows, hs], kb], axis=1) for hs in heads]
        sts = [_dot_nt(ks[h], qs[n]) for n, (h, _) in enumerate(chains)]
        vts = [jnp.concatenate([vt_ref[h, j + w] for w in range(width)], axis=1) for h in range(hb)]
        return tuple(
            _flash_step_t(sts[n], vts[h], carries[n], _causal_mask_t(tq, tqc, qr.start) if diagonal else None)
            for n, (h, qr) in enumerate(chains))

    carries = lax.fori_loop(0, i // 2, lambda j, c: block(2 * j, 2, c, False), (_flash_init_t(tqc),) * len(chains))
    carries = lax.cond(i % 2 == 1, lambda c: block(i - 1, 1, c, False), lambda c: c, carries)
    carries = block(i, 1, carries, True)
    for (m, l, acc), (h, qr) in zip(carries, chains):
        o_ref[qr, heads[h]] = (acc / l).T.astype(o_ref.dtype)


def _bias_columns(pieces, hb):
    R, S = pieces[0].shape
    parts = jnp.stack(pieces, axis=-1)
    ones = jnp.ones_like(parts)
    q6 = jnp.concatenate([parts, ones], axis=-1)
    k6 = jnp.concatenate([ones, -parts], axis=-1)
    slot = jax.nn.one_hot(jnp.arange(R) % hb, hb, dtype=q6.dtype)
    qb = (q6[:, :, None, :] * slot[:, None, :, None]).reshape(R, S, 6 * hb)
    kb = k6.reshape(R // hb, hb, S, 6).transpose(0, 2, 1, 3).reshape(R // hb, S, 6 * hb)
    pad = ((0, 0), (0, 0), (0, LANES - 6 * hb))
    return jnp.pad(qb, pad), jnp.pad(kb, pad)


def _fox_prompt(q, k, v, c2_pieces, B, S, tq=512, hb=4, qsplit=1):
    H = N_HEADS
    nq = S // tq
    nh = H // hb
    qb, kb = _bias_columns(c2_pieces, hb)
    vt = v.reshape(B, nq, tq, H, HEAD_DIM).transpose(0, 3, 1, 4, 2).reshape(B * H, nq, HEAD_DIM, tq)
    return pl.pallas_call(
        functools.partial(_fox_prompt_kernel, tq=tq, hb=hb, qsplit=qsplit),
        grid=(B, nh, nq),
        in_specs=[
            pl.BlockSpec((tq, hb * HEAD_DIM), lambda b, h, i: (b * nq + i, h)),
            pl.BlockSpec((hb, tq, LANES), lambda b, h, i: (b * nh + h, i, 0)),
            pl.BlockSpec((S, hb * HEAD_DIM), lambda b, h, i: (b, h)),
            pl.BlockSpec((None, S, LANES), lambda b, h, i: (b * nh + h, 0, 0)),
            pl.BlockSpec((hb, nq, HEAD_DIM, tq), lambda b, h, i: (b * nh + h, 0, 0, 0)),
        ],
        out_specs=pl.BlockSpec((tq, hb * HEAD_DIM), lambda b, h, i: (b * nq + i, h)),
        out_shape=jax.ShapeDtypeStruct((B * S, WIDTH), BF16),
        compiler_params=_params(3),
        name="fox_prompt",
    )(q, qb, k, kb, vt)


def _causal_mask(tq, tk):
    rows = lax.broadcasted_iota(jnp.int32, (tq, tk), 0)
    cols = lax.broadcasted_iota(jnp.int32, (tq, tk), 1)
    return cols <= rows


def _softmax_update(s, m, l):
    m_new = jnp.maximum(m, jnp.max(s, axis=-1, keepdims=True))
    alpha = jnp.exp2(m - m_new)
    p = jnp.exp2(s - m_new)
    return m_new, alpha * l + jnp.sum(p, axis=-1, keepdims=True), alpha, p


def _fox_sample_kernel(q_ref, kc_ref, vc_ref, kn_ref, vn_ref, cq_ref, ckc_ref, ckn_ref, o_ref, m_scr, l_scr, acc_scr):
    j = pl.program_id(1)
    T = q_ref.shape[0]
    heads = [slice(h * HEAD_DIM, (h + 1) * HEAD_DIM) for h in range(N_HEADS)]

    @pl.when(j == 0)
    def _():
        m_scr[...] = jnp.full(m_scr.shape, NEG, F32)
        l_scr[...] = jnp.zeros(l_scr.shape, F32)
        acc_scr[...] = jnp.zeros(acc_scr.shape, F32)

    k_all = pltpu.einshape("mhd->hmd", kc_ref[...])
    v_all = pltpu.einshape("mhd->hmd", vc_ref[...])
    scores = [_dot_nt(q_ref[:, hs], k_all[h].astype(BF16)) + (cq_ref[h] - ckc_ref[h]) for h, hs in enumerate(heads)]
    stats = [_softmax_update(scores[h], m_scr[h], l_scr[h]) for h in range(N_HEADS)]
    for h, (m, l, alpha, p) in enumerate(stats):
        acc_scr[h] = alpha * acc_scr[h] + _dot(p.astype(BF16), v_all[h].astype(BF16))
        m_scr[h], l_scr[h] = m, l

    @pl.when(j == pl.num_programs(1) - 1)
    def _():
        mask = _causal_mask(T, T)
        for h, hs in enumerate(heads):
            s = jnp.where(mask, _dot_nt(q_ref[:, hs], kn_ref[:, hs]) + (cq_ref[h] - ckn_ref[h]), NEG)
            _, l, alpha, p = _softmax_update(s, m_scr[h], l_scr[h])
            acc = alpha * acc_scr[h] + _dot(p.astype(BF16), vn_ref[:, hs])
            o_ref[:, hs] = (acc / l).astype(o_ref.dtype)


def _fox_sample(q, k_new, v_new, cache_k, cache_v, c2, B, T, tk=1024):
    H = N_HEADS
    P = cache_k.shape[2]
    nk = P // tk
    c_q = c2[:, P:].reshape(B, H, T, 1)
    c_kc = c2[:, :P].reshape(B, H, nk, 1, tk)
    c_kn = c2[:, P:].reshape(B, H, 1, T)
    tok = pl.BlockSpec((T, WIDTH), lambda b, j: (b, 0))
    cache = pl.BlockSpec((None, None, tk, H, HEAD_DIM), lambda b, j: (0, b, j, 0, 0))
    return pl.pallas_call(
        _fox_sample_kernel,
        grid=(B, nk),
        in_specs=[
            tok, cache, cache, tok, tok,
            pl.BlockSpec((None, H, T, 1), lambda b, j: (b, 0, 0, 0)),
            pl.BlockSpec((None, H, None, 1, tk), lambda b, j: (b, 0, j, 0, 0)),
            pl.BlockSpec((None, H, 1, T), lambda b, j: (b, 0, 0, 0)),
        ],
        out_specs=tok,
        out_shape=jax.ShapeDtypeStruct((B * T, WIDTH), BF16),
        scratch_shapes=[pltpu.VMEM((H, T, 1), F32), pltpu.VMEM((H, T, 1), F32), pltpu.VMEM((H, T, HEAD_DIM), F32)],
        compiler_params=_params(2),
        name="fox_sample",
    )(q, cache_k, cache_v, k_new, v_new, c_q, c_kc, c_kn)


def _retention_kernel(q_ref, k_ref, v_ref, g_ref, gn_ref, s0_ref, dm_ref, qd_ref, kd_ref, cd_ref,
                      o_ref, s_ref, *, L, hb, unroll):
    T = q_ref.shape[0]

    def body(it, states):
        states = list(states)
        work = []
        for u in range(unroll):
            rows = pl.ds(pl.multiple_of((it * unroll + u) * L, L), L)
            for h in range(hb):
                hs = slice(h * HEAD_DIM, (h + 1) * HEAD_DIM)
                q = q_ref[rows, hs]
                k = k_ref[rows, hs]
                v = v_ref[rows, hs].astype(BF16)
                att = _dot_nt(q, k) * dm_ref[h]
                inner = _dot(att.astype(BF16), v)
                update = _dot((k.astype(F32) * kd_ref[h]).T.astype(BF16), v)
                work.append((h, rows, hs, inner, update, (q.astype(F32) * qd_ref[h]).astype(BF16)))
        outs = []
        for h, rows, hs, inner, update, q_decayed in work:
            outs.append((rows, hs, inner + _dot(q_decayed, states[h].astype(BF16))))
            states[h] = states[h] * cd_ref[h] + update
        for rows, hs, o in outs:
            mu = jnp.mean(o, axis=-1, keepdims=True)
            oc = o - mu
            y = oc * lax.rsqrt(jnp.mean(oc * oc, axis=-1, keepdims=True) + EPS)
            o_ref[rows, hs] = (y * gn_ref[:, hs] * _silu(g_ref[rows, hs])).astype(o_ref.dtype)
        return tuple(states)

    states = lax.fori_loop(0, T // (L * unroll), body, tuple(s0_ref[h] for h in range(hb)))
    for h in range(hb):
        s_ref[h] = states[h]


def _retention_consts(L):
    H = N_HEADS
    log_g = jnp.log1p(-jnp.exp2(-5.0 - jnp.arange(H, dtype=F32)))
    i = jnp.arange(L, dtype=F32)
    diff = i[:, None] - i[None, :]
    dmat = jnp.where(diff[None] >= 0, jnp.exp(jnp.maximum(diff, 0.0)[None] * log_g[:, None, None]), 0.0)
    q_dec = jnp.exp((i + 1.0)[None, :] * log_g[:, None])
    k_dec = jnp.exp((L - 1.0 - i)[None, :] * log_g[:, None])
    c_dec = jnp.exp(L * log_g)
    rep = lambda a: jnp.broadcast_to(a[..., None], a.shape + (LANES,))
    return dmat, rep(q_dec), rep(k_dec), rep(c_dec[:, None])


def _retention(qk, vg, gn, s0, B, T):
    H = N_HEADS
    L = min(RET_CHUNK, T)
    n_chunks = T // L
    unroll = math.gcd(RET_UNROLL, n_chunks)
    hb = 1 if n_chunks > 1 else H
    nh = H // hb
    dmat, q_dec, k_dec, c_dec = _retention_consts(L)
    seq = lambda off: pl.BlockSpec((T, hb * HEAD_DIM), lambda b, h: (b, h + off))
    per_head = lambda r: pl.BlockSpec((hb, r, LANES), lambda b, h: (h, 0, 0))
    state = pl.BlockSpec((None, hb, HEAD_DIM, HEAD_DIM), lambda b, h: (b, h, 0, 0))
    return pl.pallas_call(
        functools.partial(_retention_kernel, L=L, hb=hb, unroll=unroll),
        grid=(B, nh),
        in_specs=[seq(0), seq(nh), seq(0), seq(nh),
                  pl.BlockSpec((1, hb * HEAD_DIM), lambda b, h: (0, h)), state,
                  pl.BlockSpec((hb, L, L), lambda b, h: (h, 0, 0)), per_head(L), per_head(L), per_head(1)],
        out_specs=[seq(0), state],
        out_shape=[jax.ShapeDtypeStruct((B * T, WIDTH), BF16), jax.ShapeDtypeStruct((B, H, HEAD_DIM, HEAD_DIM), F32)],
        compiler_params=_params(2),
        name="retention",
    )(qk, qk, vg, vg, gn.reshape(1, WIDTH), s0, dmat, q_dec, k_dec, c_dec)


def _conv_kernel(x_ref, prev_ref, init_ref, w_ref, b_ref, g_ref, beta_ref, o_ref, xp_ref, xs_ref, z_ref, *, tr):
    i = pl.program_id(1)

    @pl.when(i == 0)
    def _():
        xp_ref[0:HALO, :] = init_ref[...]

    @pl.when(i > 0)
    def _():
        xp_ref[0:HALO, :] = prev_ref[...]

    C = x_ref.shape[1]
    xp_ref[HALO:HALO + tr, :] = x_ref[...]
    xp_ref[HALO + tr:HALO + tr + SUBLANES, :] = jnp.zeros((SUBLANES, C), F32)
    first = HALO - (CONV_WIDTH - 1)
    span = tr + CONV_SPAN
    for r in range(SUBLANES):
        xs_ref[r] = xp_ref[pl.ds(first + r, span), :]

    groups = CONV_SPAN // SUBLANES + 1
    for c in range(C // LANES):
        cs = slice(c * LANES, (c + 1) * LANES)
        wts = [w_ref[w, :, cs] for w in range(CONV_WIDTH)]
        bias = jnp.broadcast_to(b_ref[:, cs], (SUBLANES, LANES))

        def step(u, accs, cs=cs, wts=wts, bias=bias):
            accs = (bias,) + accs
            r0 = pl.multiple_of(u * SUBLANES, SUBLANES)
            for r in range(SUBLANES):
                x = xs_ref[r, pl.ds(r0, SUBLANES), cs]
                accs = tuple(acc + x * wts[SUBLANES * a + r] if SUBLANES * a + r < CONV_WIDTH else acc
                             for a, acc in enumerate(accs))
            z_ref[pl.ds(r0, SUBLANES), cs] = accs[-1]
            return accs[:-1]

        lax.fori_loop(0, span // SUBLANES, step, (bias,) * (groups - 1), unroll=CONV_UNROLL)

    z = z_ref[CONV_SPAN:CONV_SPAN + tr, :]
    mu = jnp.mean(z, axis=-1, keepdims=True)
    zc = z - mu
    y = zc * lax.rsqrt(jnp.mean(zc * zc, axis=-1, keepdims=True) + EPS) * g_ref[...] + beta_ref[...]
    o_ref[...] = _silu(y).astype(o_ref.dtype)


def _conv_module(glu, init, w_dw, b_dw, ln_g, ln_b, B, T, tr):
    C = glu.shape[1]
    nt = T // tr
    per = tr // HALO
    w_rep = jnp.broadcast_to(jnp.pad(w_dw, ((0, HALO - CONV_WIDTH), (0, 0)))[:, None, :], (HALO, SUBLANES, C))
    vec = pl.BlockSpec((1, C), lambda b, i: (0, 0))
    return pl.pallas_call(
        functools.partial(_conv_kernel, tr=tr),
        grid=(B, nt),
        in_specs=[
            pl.BlockSpec((tr, C), lambda b, i: (b * nt + i, 0)),
            pl.BlockSpec((HALO, C), lambda b, i: (b * nt * per + jnp.maximum(i * per - 1, 0), 0)),
            pl.BlockSpec((None, HALO, C), lambda b, i: (b, 0, 0)),
            pl.BlockSpec((HALO, SUBLANES, C), lambda b, i: (0, 0, 0)),
            vec, vec, vec,
        ],
        out_specs=pl.BlockSpec((tr, C), lambda b, i: (b * nt + i, 0)),
        out_shape=jax.ShapeDtypeStruct((B * T, C), BF16),
        scratch_shapes=[pltpu.VMEM((HALO + tr + SUBLANES, C), F32),
                        pltpu.VMEM((SUBLANES, tr + CONV_SPAN, C), F32),
                        pltpu.VMEM((tr + CONV_SPAN, C), F32)],
        compiler_params=_params(2),
        name="conv_module",
    )(glu, glu, init, w_rep, b_dw.reshape(1, C), ln_g.reshape(1, C), ln_b.reshape(1, C))


def _rope_tables(pos):
    half = HEAD_DIM // 2
    inv = jnp.exp(-math.log(ROPE_BASE) * jnp.arange(half, dtype=F32) / half)
    ang = pos.astype(F32)[:, None] * inv[None, :]
    cos, sin = jnp.cos(ang), jnp.sin(ang)
    return jnp.concatenate([cos, cos], axis=-1), jnp.concatenate([-sin, sin], axis=-1)


def _prepare_weights(norm_mix, norm_ffn, w_in, b_forget, q_norm_gain, k_norm_gain, ret_norm_gain, w_out,
                     w_pw1, b_pw1, w_dw, b_dw, conv_ln_gain, conv_ln_bias, w_pw2, b_pw2,
                     w_ffn_gate, w_ffn_up, w_ffn_down):
    W = WIDTH
    D = w_in.shape[1]
    f0 = 3 * W
    f1 = f0 + N_HEADS
    w_b = w_in[0, :, f1:]
    return dict(
        norm_mix=norm_mix, norm_ffn=norm_ffn,
        w_q=_Rhs(w_in, 0, D, col=0), w_k=_Rhs(w_in, 0, D, col=W), w_v=_Rhs(w_in, 0, D, col=2 * W),
        w_f=_Rhs(w_in, 0, D, col=f0),
        b_f=jnp.pad(b_forget[0], (0, LANES - N_HEADS)).reshape(1, LANES),
        w_qkb=_Rhs(w_b, None, D, col=0), w_vgb=_Rhs(w_b, None, D, col=2 * W),
        rope_scale=jnp.concatenate([jnp.ones((1, W), F32), jnp.full((1, W), HEAD_DIM ** -0.5, F32)], axis=1),
        gq=q_norm_gain[0].reshape(1, HEAD_DIM), gk=k_norm_gain[0].reshape(1, HEAD_DIM), gn=ret_norm_gain[0],
        w_oa=_Rhs(w_out, 0, W, k_blk=0), w_ob=_Rhs(w_out, 0, W, k_blk=1),
        w_1a=_Rhs(w_pw1, 0, D, col=0), w_1g=_Rhs(w_pw1, 0, D, col=w_pw1.shape[2] // 2),
        b_1a=b_pw1[0, :2 * W].reshape(1, -1), b_1g=b_pw1[0, 2 * W:].reshape(1, -1),
        w_dw=w_dw[0], b_dw=b_dw[0], ln_g=conv_ln_gain[0], ln_b=conv_ln_bias[0],
        w_2=_Rhs(w_pw2, 0, w_pw2.shape[1]), b_2=b_pw2[0].reshape(1, -1),
        w_g=[_Rhs(w_ffn_gate, l, D) for l in range(2)], w_u=[_Rhs(w_ffn_up, l, D) for l in range(2)],
        w_d=[_Rhs(w_ffn_down, l, w_ffn_down.shape[1]) for l in range(2)],
        d_ff=w_ffn_gate.shape[2],
    )


def _ffn(x, P, layer, tag):
    D = x.shape[1]
    h = _rmsnorm(x, P["norm_ffn"][layer], f"rms_ffn{layer}_{tag}")
    act, = _matmul(f"ffn_up{layer}_{tag}", [h], [P["w_g"][layer], P["w_u"][layer]], P["d_ff"],
                   [[(0, 0)], [(0, 1)]], _ep_swiglu, [BF16], tm=1024)
    out, = _matmul(f"ffn_down{layer}_{tag}", [act], [P["w_d"][layer]], D, [[(0, 0)]], _ep_residual, [F32],
                   tile=[x])
    return out


def _trunk(x3, pos, past, conv_past, P, tag):
    B, T, D = x3.shape
    M = B * T
    H = N_HEADS
    W = WIDTH
    x = x3.reshape(M, D)
    one = [[(0, 0)]]
    tall = dict(tm=1024, slabs=EPILOGUE_SLABS)
    wide = dict(tm=1024, tn=1024)
    busy = dict(wide, slabs=EPILOGUE_SLABS)

    h = _rmsnorm(x, P["norm_mix"][0], f"rms_mix0_{tag}")
    q, = _matmul(f"proj_q_{tag}", [h], [P["w_q"]], W, one,
                 functools.partial(_ep_headnorm, scales=(HEAD_DIM ** -0.5 * LOG2E,)), [BF16], full=[P["gq"]], **busy)
    k32, k16 = _matmul(f"proj_k_{tag}", [h], [P["w_k"]], W, one,
                       functools.partial(_ep_headnorm, scales=(1.0, 1.0)), [F32, BF16], full=[P["gk"]], **busy)
    v32, v16 = _matmul(f"proj_v_{tag}", [h], [P["w_v"]], W, one, _ep_identity, [F32, BF16], **wide)
    logf_pad, = _matmul(f"proj_f_{tag}", [h], [P["w_f"]], LANES, one, _ep_logsigmoid, [F32], col=[P["b_f"]],
                        tn=LANES)
    cos, sin = _rope_tables(pos)
    reps = max(1, wide["tm"] // T)
    cos, sin = jnp.tile(cos, (reps, 1)), jnp.tile(sin, (reps, 1))
    qk_b, = _matmul(f"proj_qkb_{tag}", [h], [P["w_qkb"]], 2 * W, one, _ep_rope, [BF16], col=[P["rope_scale"]],
                    row=[cos, sin], **busy)
    vg_b, = _matmul(f"proj_vgb_{tag}", [h], [P["w_vgb"]], 2 * W, one, _ep_identity, [F32], **wide)

    logf = logf_pad[:, :H].reshape(B, T, H)
    lf_t = logf.transpose(0, 2, 1).reshape(B * H, T)
    if past is None:
        _, c2_pieces = _cumsum_rows(lf_t, f"cumsum_{tag}", LOG2E)
        oa = _fox_prompt(q, k16, v16, c2_pieces, B, T)
        s0 = jnp.zeros((B, H, HEAD_DIM, HEAD_DIM), F32)
    else:
        kc, vc, lfc, s0 = past
        Pl = kc.shape[2]
        lf_all = jnp.concatenate([lfc.transpose(0, 2, 1).reshape(B * H, Pl), lf_t], axis=1)
        pad = (-lf_all.shape[1]) % LANES
        c2, _ = _cumsum_rows(jnp.pad(lf_all, ((0, 0), (0, pad))), f"cumsum_{tag}", LOG2E)
        oa = _fox_sample(q, k16, v16, kc, vc, c2[:, :Pl + T], B, T)
    ob, s_new = _retention(qk_b, vg_b, P["gn"], s0, B, T)
    x, = _matmul(f"out_proj_{tag}", [oa, ob], [P["w_oa"], P["w_ob"]], D, [[(0, 0), (1, 1)]], _ep_residual, [F32],
                 tile=[x], **wide)
    x = _ffn(x, P, 0, tag)

    h = _rmsnorm(x, P["norm_mix"][1], f"rms_mix1_{tag}")
    glu, = _matmul(f"conv_pw1_{tag}", [h], [P["w_1a"], P["w_1g"]], D, [[(0, 0)], [(0, 1)]], _ep_glu, [F32],
                   col=[P["b_1a"], P["b_1g"]], **tall)
    keep = CONV_WIDTH - 1
    assert T >= keep
    if conv_past is None:
        init = jnp.zeros((B, HALO, D), F32)
    else:
        init = jnp.pad(conv_past, ((0, 0), (HALO - keep, 0), (0, 0)))
    z = _conv_module(glu, init, P["w_dw"], P["b_dw"], P["ln_g"], P["ln_b"], B, T, tr=min(256, T))
    conv_new = glu.reshape(B, T, D)[:, -keep:]
    x, = _matmul(f"conv_pw2_{tag}", [z], [P["w_2"]], D, one, _ep_bias_residual, [F32], tile=[x], col=[P["b_2"]],
                 **wide)
    x = _ffn(x, P, 1, tag)

    return (x.reshape(B, T, D), k32.reshape(1, B, T, H, HEAD_DIM), v32.reshape(1, B, T, H, HEAD_DIM),
            logf[None], s_new[None], conv_new[None])


def kernel(x_prompt, x_sample, cache_k, cache_v, cache_logf, state_ret, state_conv,
           norm_mix, norm_ffn, w_in, b_forget, q_norm_gain, k_norm_gain, ret_norm_gain, w_out,
           w_pw1, b_pw1, w_dw, b_dw, conv_ln_gain, conv_ln_bias, w_pw2, b_pw2,
           w_ffn_gate, w_ffn_up, w_ffn_down):
    P = _prepare_weights(norm_mix, norm_ffn, w_in, b_forget, q_norm_gain, k_norm_gain, ret_norm_gain, w_out,
                         w_pw1, b_pw1, w_dw, b_dw, conv_ln_gain, conv_ln_bias, w_pw2, b_pw2,
                         w_ffn_gate, w_ffn_up, w_ffn_down)
    pos_p = jnp.arange(x_prompt.shape[1])
    y_p, k_p, v_p, lf_p, r_p, c_p = _trunk(x_prompt, pos_p, None, None, P, "p")
    _, Bs, Pl, H, hd = cache_k.shape
    pos_s = Pl + jnp.arange(x_sample.shape[1])
    past = (cache_k, cache_v, cache_logf.reshape(Bs, Pl, H), state_ret.reshape(Bs, H, hd, hd))
    conv_past = state_conv.reshape(state_conv.shape[1:])
    y_s, k_s, v_s, lf_s, r_s, c_s = _trunk(x_sample, pos_s, past, conv_past, P, "s")
    return (y_p, y_s, k_p, v_p, lf_p, r_p, c_p, k_s, v_s, lf_s, r_s, c_s)
```

```python
import functools
import math
from typing import NamedTuple, Optional

import jax
import jax.numpy as jnp
from jax import lax
from jax.experimental import pallas as pl
from jax.experimental.pallas import tpu as pltpu

F32 = jnp.float32
BF16 = jnp.bfloat16

LANES = 128
SUBLANES = 8
HALO = 32
VMEM_LIMIT = 56 * 1024 * 1024

HEAD_DIM = 128
N_HEADS = 8
WIDTH = N_HEADS * HEAD_DIM
CONV_WIDTH = 31
ROPE_BASE = 10000.0
EPS = 1e-6
NEG = -1e30
LOG2E = math.log2(math.e)
CONV_SPAN = (CONV_WIDTH - 1) // SUBLANES * SUBLANES
CONV_UNROLL = 5
RET_CHUNK = 256
EPILOGUE_SLABS = 4
RET_UNROLL = 4


def _params(n_grid):
    return pltpu.CompilerParams(dimension_semantics=("arbitrary",) * n_grid, vmem_limit_bytes=VMEM_LIMIT)


def _silu(x):
    return x * jax.nn.sigmoid(x)


def _dot(a, b):
    return jnp.dot(a, b, preferred_element_type=F32)


def _dot_nt(a, b):
    return lax.dot_general(a, b, (((1,), (1,)), ((), ())), preferred_element_type=F32)


def _rms_kernel(x_ref, g_ref, o_ref):
    x = x_ref[...]
    ms = jnp.mean(x * x, axis=-1, keepdims=True)
    o_ref[...] = (x * lax.rsqrt(ms + EPS) * g_ref[...]).astype(o_ref.dtype)


def _rmsnorm(x, g, name, tm=512):
    M, D = x.shape
    return pl.pallas_call(
        _rms_kernel,
        grid=(M // tm,),
        in_specs=[pl.BlockSpec((tm, D), lambda i: (i, 0)), pl.BlockSpec((1, D), lambda i: (0, 0))],
        out_specs=pl.BlockSpec((tm, D), lambda i: (i, 0)),
        out_shape=jax.ShapeDtypeStruct((M, D), BF16),
        compiler_params=_params(1),
        name=name,
    )(x, g.reshape(1, D))


class _Rhs(NamedTuple):
    arr: jax.Array
    lead: Optional[int]
    k: int
    k_blk: int = 0
    col: int = 0


def _mm_kernel(*refs, n_lhs, n_rhs, n_extra, n_out, row_blocked, slabs, products, epilogue):
    lhs = refs[:n_lhs]
    rhs = refs[n_lhs:n_lhs + n_rhs]
    extras = refs[n_lhs + n_rhs:n_lhs + n_rhs + n_extra]
    outs = refs[n_lhs + n_rhs + n_extra:n_lhs + n_rhs + n_extra + n_out]
    wbuf = refs[n_lhs + n_rhs + n_extra + n_out:]

    @pl.when(pl.program_id(1) == 0)
    def _():
        for w, b in zip(rhs, wbuf):
            b[...] = w[...].astype(BF16)

    ts = lhs[0].shape[0] // slabs

    def slab_dots(s):
        rows = pl.ds(s * ts, ts)
        accs = []
        for prod in products:
            acc = None
            for a, b in prod:
                d = _dot(lhs[a][rows, :], wbuf[b][...])
                acc = d if acc is None else acc + d
            accs.append(acc)
        return accs

    def slab_epilogue(s, accs):
        rows = pl.ds(s * ts, ts)
        epilogue(accs, [e.at[rows, :] if blocked else e for e, blocked in zip(extras, row_blocked)],
                 [o.at[rows, :] for o in outs])

    pending = slab_dots(0)
    for s in range(1, slabs):
        nxt = slab_dots(s)
        slab_epilogue(s - 1, pending)
        pending = nxt
    slab_epilogue(slabs - 1, pending)


def _matmul(name, lhs, rhs, n_cols, products, epilogue, out_dtypes, *, tile=(), col=(), row=(), full=(),
            tm=512, tn=512, slabs=1):
    M = lhs[0].shape[0]
    assert M % tm == 0 and n_cols % tn == 0
    in_specs = [pl.BlockSpec((tm, a.shape[1]), lambda j, i: (i, 0)) for a in lhs]
    for w in rhs:
        assert w.col % tn == 0
        c0 = w.col // tn
        if w.lead is None:
            in_specs.append(pl.BlockSpec((w.k, tn), lambda j, i, kb=w.k_blk, c0=c0: (kb, j + c0)))
        else:
            in_specs.append(pl.BlockSpec((None, w.k, tn),
                                         lambda j, i, ld=w.lead, kb=w.k_blk, c0=c0: (ld, kb, j + c0)))
    in_specs += [pl.BlockSpec((tm, tn), lambda j, i: (i, j)) for _ in tile]
    in_specs += [pl.BlockSpec((1, tn), lambda j, i: (0, j)) for _ in col]
    for r in row:
        assert r.shape[0] % tm == 0
        in_specs.append(pl.BlockSpec((tm, r.shape[1]), lambda j, i, nblk=r.shape[0] // tm: (i % nblk, 0)))
    in_specs += [pl.BlockSpec(f.shape, lambda j, i, nd=f.ndim: (0,) * nd) for f in full]
    extras = tuple(tile) + tuple(col) + tuple(row) + tuple(full)
    row_blocked = (True,) * len(tile) + (False,) * len(col) + (True,) * len(row) + (False,) * len(full)
    kern = functools.partial(_mm_kernel, n_lhs=len(lhs), n_rhs=len(rhs), n_extra=len(extras),
                             n_out=len(out_dtypes), row_blocked=row_blocked, slabs=slabs, products=products,
                             epilogue=epilogue)
    return pl.pallas_call(
        kern,
        grid=(n_cols // tn, M // tm),
        in_specs=in_specs,
        out_specs=[pl.BlockSpec((tm, tn), lambda j, i: (i, j)) for _ in out_dtypes],
        out_shape=[jax.ShapeDtypeStruct((M, n_cols), dt) for dt in out_dtypes],
        scratch_shapes=[pltpu.VMEM((w.k, tn), BF16) for w in rhs],
        compiler_params=_params(2),
        name=name,
    )(*lhs, *[w.arr for w in rhs], *extras)


def _ep_headnorm(accs, extras, outs, *, scales):
    gain = extras[0][...]
    y = accs[0]
    for h in range(y.shape[1] // HEAD_DIM):
        sl = slice(h * HEAD_DIM, (h + 1) * HEAD_DIM)
        yh = y[:, sl]
        ms = jnp.mean(yh * yh, axis=-1, keepdims=True)
        r = yh * lax.rsqrt(ms + EPS) * gain
        for o, sc in zip(outs, scales):
            o[:, sl] = (r if sc == 1.0 else r * sc).astype(o.dtype)


def _ep_identity(accs, extras, outs):
    for o in outs:
        o[...] = accs[0].astype(o.dtype)


def _ep_logsigmoid(accs, extras, outs):
    z = accs[0] + extras[0][...]
    outs[0][...] = -(jnp.maximum(-z, 0.0) + jnp.log1p(jnp.exp(-jnp.abs(z))))


def _ep_rope(accs, extras, outs):
    scale, cos, sin = extras[0][...], extras[1][...], extras[2][...]
    y = accs[0]
    for h in range(y.shape[1] // HEAD_DIM):
        sl = slice(h * HEAD_DIM, (h + 1) * HEAD_DIM)
        yh = y[:, sl]
        r = yh * cos + pltpu.roll(yh, HEAD_DIM // 2, 1) * sin
        outs[0][:, sl] = (r * scale[:, sl]).astype(outs[0].dtype)


def _ep_residual(accs, extras, outs):
    outs[0][...] = extras[0][...] + accs[0]


def _ep_bias_residual(accs, extras, outs):
    outs[0][...] = extras[0][...] + (accs[0] + extras[1][...])


def _ep_swiglu(accs, extras, outs):
    outs[0][...] = (_silu(accs[0]) * accs[1]).astype(outs[0].dtype)


def _ep_glu(accs, extras, outs):
    a = accs[0] + extras[0][...]
    g = accs[1] + extras[1][...]
    outs[0][...] = a * jax.nn.sigmoid(g)


def _split3(x):
    hi = x.astype(BF16)
    r1 = x - hi.astype(F32)
    mid = r1.astype(BF16)
    lo = (r1 - mid.astype(F32)).astype(BF16)
    return hi, mid, lo


def _cumsum_kernel(x_ref, c_ref, hi_ref, mid_ref, lo_ref, *, scale):
    nblk, R, _ = x_ref.shape
    row = lax.broadcasted_iota(jnp.int32, (LANES, LANES), 0)
    col = lax.broadcasted_iota(jnp.int32, (LANES, LANES), 1)
    tri = jnp.where(row <= col, 1.0, 0.0).astype(BF16)
    ones = jnp.ones((LANES, LANES), BF16)

    def body(b, carry):
        hi, mid, lo = _split3(x_ref[b])
        within = _dot(hi, tri) + _dot(mid, tri) + _dot(lo, tri)
        total = _dot(hi, ones) + _dot(mid, ones) + _dot(lo, ones)
        c = (carry + within) * scale
        c_ref[b] = c
        hi_ref[b], mid_ref[b], lo_ref[b] = _split3(c)
        return carry + total

    lax.fori_loop(0, nblk, body, jnp.zeros((R, LANES), F32))


def _cumsum_rows(x, name, scale):
    R, N = x.shape
    nblk = N // LANES
    x3 = x.reshape(R, nblk, LANES).transpose(1, 0, 2)
    outs = pl.pallas_call(
        functools.partial(_cumsum_kernel, scale=scale),
        out_shape=[jax.ShapeDtypeStruct((nblk, R, LANES), dt) for dt in (F32, BF16, BF16, BF16)],
        compiler_params=pltpu.CompilerParams(vmem_limit_bytes=VMEM_LIMIT),
        name=name,
    )(x3)
    c, hi, mid, lo = [o.transpose(1, 0, 2).reshape(R, N) for o in outs]
    return c, (hi, mid, lo)


def _flash_step_t(st, vt, carry, mask):
    m, l, acc = carry
    if mask is not None:
        st = jnp.where(mask, st, NEG)
    m_new = jnp.maximum(m, jnp.max(st, axis=0, keepdims=True))
    alpha = jnp.exp2(m - m_new)
    pt = jnp.exp2(st - m_new)
    l = alpha * l + jnp.sum(pt, axis=0, keepdims=True)
    acc = alpha * acc + _dot(vt, pt.astype(BF16))
    return m_new, l, acc


def _flash_init_t(tq):
    return (jnp.full((1, tq), NEG, F32), jnp.zeros((1, tq), F32), jnp.zeros((HEAD_DIM, tq), F32))


def _causal_mask_t(tk, tq, q_off):
    keys = lax.broadcasted_iota(jnp.int32, (tk, tq), 0)
    queries = lax.broadcasted_iota(jnp.int32, (tk, tq), 1) + q_off
    return keys <= queries


def _fox_prompt_kernel(q_ref, qb_ref, k_ref, kb_ref, vt_ref, o_ref, *, tq, hb, qsplit):
    i = pl.program_id(2)
    tqc = tq // qsplit
    heads = [slice(h * HEAD_DIM, (h + 1) * HEAD_DIM) for h in range(hb)]
    chains = [(h, slice(c * tqc, (c + 1) * tqc)) for h in range(hb) for c in range(qsplit)]
    qs = [jnp.concatenate([q_ref[qr, heads[h]], qb_ref[h, qr, :]], axis=1) for h, qr in chains]

    def block(j, width, carries, diagonal):
        rows = pl.ds(pl.multiple_of(j * tq, tq), width * tq)
        kb = kb_ref[rows, :]
        ks = [jnp.concatenate([k_ref[rows, hs], kb], axis=1) for hs in heads]
        sts = [_dot_nt(ks[h], qs[n]) for n, (h, _) in enumerate(chains)]
        vts = [jnp.concatenate([vt_ref[h, j + w] for w in range(width)], axis=1) for h in range(hb)]
        return tuple(
            _flash_step_t(sts[n], vts[h], carries[n], _causal_mask_t(tq, tqc, qr.start) if diagonal else None)
            for n, (h, qr) in enumerate(chains))

    carries = lax.fori_loop(0, i // 2, lambda j, c: block(2 * j, 2, c, False), (_flash_init_t(tqc),) * len(chains))
    carries = lax.cond(i % 2 == 1, lambda c: block(i - 1, 1, c, False), lambda c: c, carries)
    carries = block(i, 1, carries, True)
    for (m, l, acc), (h, qr) in zip(carries, chains):
        o_ref[qr, heads[h]] = (acc / l).T.astype(o_ref.dtype)


def _bias_columns(pieces, hb):
    R, S = pieces[0].shape
    parts = jnp.stack(pieces, axis=-1)
    ones = jnp.ones_like(parts)
    q6 = jnp.concatenate([parts, ones], axis=-1)
    k6 = jnp.concatenate([ones, -parts], axis=-1)
    slot = jax.nn.one_hot(jnp.arange(R) % hb, hb, dtype=q6.dtype)
    qb = (q6[:, :, None, :] * slot[:, None, :, None]).reshape(R, S, 6 * hb)
    kb = k6.reshape(R // hb, hb, S, 6).transpose(0, 2, 1, 3).reshape(R // hb, S, 6 * hb)
    pad = ((0, 0), (0, 0), (0, LANES - 6 * hb))
    return jnp.pad(qb, pad), jnp.pad(kb, pad)


def _fox_prompt(q, k, v, c2_pieces, B, S, tq=512, hb=4, qsplit=1):
    H = N_HEADS
    nq = S // tq
    nh = H // hb
    qb, kb = _bias_columns(c2_pieces, hb)
    vt = v.reshape(B, nq, tq, H, HEAD_DIM).transpose(0, 3, 1, 4, 2).reshape(B * H, nq, HEAD_DIM, tq)
    return pl.pallas_call(
        functools.partial(_fox_prompt_kernel, tq=tq, hb=hb, qsplit=qsplit),
        grid=(B, nh, nq),
        in_specs=[
            pl.BlockSpec((tq, hb * HEAD_DIM), lambda b, h, i: (b * nq + i, h)),
            pl.BlockSpec((hb, tq, LANES), lambda b, h, i: (b * nh + h, i, 0)),
            pl.BlockSpec((S, hb * HEAD_DIM), lambda b, h, i: (b, h)),
            pl.BlockSpec((None, S, LANES), lambda b, h, i: (b * nh + h, 0, 0)),
            pl.BlockSpec((hb, nq, HEAD_DIM, tq), lambda b, h, i: (b * nh + h, 0, 0, 0)),
        ],
        out_specs=pl.BlockSpec((tq, hb * HEAD_DIM), lambda b, h, i: (b * nq + i, h)),
        out_shape=jax.ShapeDtypeStruct((B * S, WIDTH), BF16),
        compiler_params=_params(3),
        name="fox_prompt",
    )(q, qb, k, kb, vt)


def _causal_mask(tq, tk):
    rows = lax.broadcasted_iota(jnp.int32, (tq, tk), 0)
    cols = lax.broadcasted_iota(jnp.int32, (tq, tk), 1)
    return cols <= rows


def _softmax_update(s, m, l):
    m_new = jnp.maximum(m, jnp.max(s, axis=-1, keepdims=True))
    alpha = jnp.exp2(m - m_new)
    p = jnp.exp2(s - m_new)
    return m_new, alpha * l + jnp.sum(p, axis=-1, keepdims=True), alpha, p


def _fox_sample_kernel(q_ref, kc_ref, vc_ref, kn_ref, vn_ref, cq_ref, ckc_ref, ckn_ref, o_ref, m_scr, l_scr, acc_scr):
    j = pl.program_id(1)
    T = q_ref.shape[0]
    heads = [slice(h * HEAD_DIM, (h + 1) * HEAD_DIM) for h in range(N_HEADS)]

    @pl.when(j == 0)
    def _():
        m_scr[...] = jnp.full(m_scr.shape, NEG, F32)
        l_scr[...] = jnp.zeros(l_scr.shape, F32)
        acc_scr[...] = jnp.zeros(acc_scr.shape, F32)

    k_all = pltpu.einshape("mhd->hmd", kc_ref[...])
    v_all = pltpu.einshape("mhd->hmd", vc_ref[...])
    scores = [_dot_nt(q_ref[:, hs], k_all[h].astype(BF16)) + (cq_ref[h] - ckc_ref[h]) for h, hs in enumerate(heads)]
    stats = [_softmax_update(scores[h], m_scr[h], l_scr[h]) for h in range(N_HEADS)]
    for h, (m, l, alpha, p) in enumerate(stats):
        acc_scr[h] = alpha * acc_scr[h] + _dot(p.astype(BF16), v_all[h].astype(BF16))
        m_scr[h], l_scr[h] = m, l

    @pl.when(j == pl.num_programs(1) - 1)
    def _():
        mask = _causal_mask(T, T)
        for h, hs in enumerate(heads):
            s = jnp.where(mask, _dot_nt(q_ref[:, hs], kn_ref[:, hs]) + (cq_ref[h] - ckn_ref[h]), NEG)
            _, l, alpha, p = _softmax_update(s, m_scr[h], l_scr[h])
            acc = alpha * acc_scr[h] + _dot(p.astype(BF16), vn_ref[:, hs])
            o_ref[:, hs] = (acc / l).astype(o_ref.dtype)


def _fox_sample(q, k_new, v_new, cache_k, cache_v, c2, B, T, tk=1024):
    H = N_HEADS
    P = cache_k.shape[2]
    nk = P // tk
    c_q = c2[:, P:].reshape(B, H, T, 1)
    c_kc = c2[:, :P].reshape(B, H, nk, 1, tk)
    c_kn = c2[:, P:].reshape(B, H, 1, T)
    tok = pl.BlockSpec((T, WIDTH), lambda b, j: (b, 0))
    cache = pl.BlockSpec((None, None, tk, H, HEAD_DIM), lambda b, j: (0, b, j, 0, 0))
    return pl.pallas_call(
        _fox_sample_kernel,
        grid=(B, nk),
        in_specs=[
            tok, cache, cache, tok, tok,
            pl.BlockSpec((None, H, T, 1), lambda b, j: (b, 0, 0, 0)),
            pl.BlockSpec((None, H, None, 1, tk), lambda b, j: (b, 0, j, 0, 0)),
            pl.BlockSpec((None, H, 1, T), lambda b, j: (b, 0, 0, 0)),
        ],
        out_specs=tok,
        out_shape=jax.ShapeDtypeStruct((B * T, WIDTH), BF16),
        scratch_shapes=[pltpu.VMEM((H, T, 1), F32), pltpu.VMEM((H, T, 1), F32), pltpu.VMEM((H, T, HEAD_DIM), F32)],
        compiler_params=_params(2),
        name="fox_sample",
    )(q, cache_k, cache_v, k_new, v_new, c_q, c_kc, c_kn)


def _retention_kernel(q_ref, k_ref, v_ref, g_ref, gn_ref, s0_ref, dm_ref, qd_ref, kd_ref, cd_ref,
                      o_ref, s_ref, *, L, hb, unroll):
    T = q_ref.shape[0]

    def body(it, states):
        states = list(states)
        work = []
        for u in range(unroll):
            rows = pl.ds(pl.multiple_of((it * unroll + u) * L, L), L)
            for h in range(hb):
                hs = slice(h * HEAD_DIM, (h + 1) * HEAD_DIM)
                q = q_ref[rows, hs]
                k = k_ref[rows, hs]
                v = v_ref[rows, hs].astype(BF16)
                att = _dot_nt(q, k) * dm_ref[h]
                inner = _dot(att.astype(BF16), v)
                update = _dot((k.astype(F32) * kd_ref[h]).T.astype(BF16), v)
                work.append((h, rows, hs, inner, update, (q.astype(F32) * qd_ref[h]).astype(BF16)))
        outs = []
        for h, rows, hs, inner, update, q_decayed in work:
            outs.append((rows, hs, inner + _dot(q_decayed, states[h].astype(BF16))))
            states[h] = states[h] * cd_ref[h] + update
        for rows, hs, o in outs:
            mu = jnp.mean(o, axis=-1, keepdims=True)
            oc = o - mu
            y = oc * lax.rsqrt(jnp.mean(oc * oc, axis=-1, keepdims=True) + EPS)
            o_ref[rows, hs] = (y * gn_ref[:, hs] * _silu(g_ref[rows, hs])).astype(o_ref.dtype)
        return tuple(states)

    states = lax.fori_loop(0, T // (L * unroll), body, tuple(s0_ref[h] for h in range(hb)))
    for h in range(hb):
        s_ref[h] = states[h]


def _retention_consts(L):
    H = N_HEADS
    log_g = jnp.log1p(-jnp.exp2(-5.0 - jnp.arange(H, dtype=F32)))
    i = jnp.arange(L, dtype=F32)
    diff = i[:, None] - i[None, :]
    dmat = jnp.where(diff[None] >= 0, jnp.exp(jnp.maximum(diff, 0.0)[None] * log_g[:, None, None]), 0.0)
    q_dec = jnp.exp((i + 1.0)[None, :] * log_g[:, None])
    k_dec = jnp.exp((L - 1.0 - i)[None, :] * log_g[:, None])
    c_dec = jnp.exp(L * log_g)
    rep = lambda a: jnp.broadcast_to(a[..., None], a.shape + (LANES,))
    return dmat, rep(q_dec), rep(k_dec), rep(c_dec[:, None])


def _retention(qk, vg, gn, s0, B, T):
    H = N_HEADS
    L = min(RET_CHUNK, T)
    n_chunks = T // L
    unroll = math.gcd(RET_UNROLL, n_chunks)
    hb = 1 if n_chunks > 1 else H
    nh = H // hb
    dmat, q_dec, k_dec, c_dec = _retention_consts(L)
    seq = lambda off: pl.BlockSpec((T, hb * HEAD_DIM), lambda b, h: (b, h + off))
    per_head = lambda r: pl.BlockSpec((hb, r, LANES), lambda b, h: (h, 0, 0))
    state = pl.BlockSpec((None, hb, HEAD_DIM, HEAD_DIM), lambda b, h: (b, h, 0, 0))
    return pl.pallas_call(
        functools.partial(_retention_kernel, L=L, hb=hb, unroll=unroll),
        grid=(B, nh),
        in_specs=[seq(0), seq(nh), seq(0), seq(nh),
                  pl.BlockSpec((1, hb * HEAD_DIM), lambda b, h: (0, h)), state,
                  pl.BlockSpec((hb, L, L), lambda b, h: (h, 0, 0)), per_head(L), per_head(L), per_head(1)],
        out_specs=[seq(0), state],
        out_shape=[jax.ShapeDtypeStruct((B * T, WIDTH), BF16), jax.ShapeDtypeStruct((B, H, HEAD_DIM, HEAD_DIM), F32)],
        compiler_params=_params(2),
        name="retention",
    )(qk, qk, vg, vg, gn.reshape(1, WIDTH), s0, dmat, q_dec, k_dec, c_dec)


def _conv_kernel(x_ref, prev_ref, init_ref, w_ref, b_ref, g_ref, beta_ref, o_ref, xp_ref, xs_ref, z_ref, *, tr):
    i = pl.program_id(1)

    @pl.when(i == 0)
    def _():
        xp_ref[0:HALO, :] = init_ref[...]

    @pl.when(i > 0)
    def _():
        xp_ref[0:HALO, :] = prev_ref[...]

    C = x_ref.shape[1]
    xp_ref[HALO:HALO + tr, :] = x_ref[...]
    xp_ref[HALO + tr:HALO + tr + SUBLANES, :] = jnp.zeros((SUBLANES, C), F32)
    first = HALO - (CONV_WIDTH - 1)
    span = tr + CONV_SPAN
    for r in range(SUBLANES):
        xs_ref[r] = xp_ref[pl.ds(first + r, span), :]

    groups = CONV_SPAN // SUBLANES + 1
    for c in range(C // LANES):
        cs = slice(c * LANES, (c + 1) * LANES)
        wts = [w_ref[w, :, cs] for w in range(CONV_WIDTH)]
        bias = jnp.broadcast_to(b_ref[:, cs], (SUBLANES, LANES))

        def step(u, accs, cs=cs, wts=wts, bias=bias):
            accs = (bias,) + accs
            r0 = pl.multiple_of(u * SUBLANES, SUBLANES)
            for r in range(SUBLANES):
                x = xs_ref[r, pl.ds(r0, SUBLANES), cs]
                accs = tuple(acc + x * wts[SUBLANES * a + r] if SUBLANES * a + r < CONV_WIDTH else acc
                             for a, acc in enumerate(accs))
            z_ref[pl.ds(r0, SUBLANES), cs] = accs[-1]
            return accs[:-1]

        lax.fori_loop(0, span // SUBLANES, step, (bias,) * (groups - 1), unroll=CONV_UNROLL)

    z = z_ref[CONV_SPAN:CONV_SPAN + tr, :]
    mu = jnp.mean(z, axis=-1, keepdims=True)
    zc = z - mu
    y = zc * lax.rsqrt(jnp.mean(zc * zc, axis=-1, keepdims=True) + EPS) * g_ref[...] + beta_ref[...]
    o_ref[...] = _silu(y).astype(o_ref.dtype)


def _conv_module(glu, init, w_dw, b_dw, ln_g, ln_b, B, T, tr):
    C = glu.shape[1]
    nt = T // tr
    per = tr // HALO
    w_rep = jnp.broadcast_to(jnp.pad(w_dw, ((0, HALO - CONV_WIDTH), (0, 0)))[:, None, :], (HALO, SUBLANES, C))
    vec = pl.BlockSpec((1, C), lambda b, i: (0, 0))
    return pl.pallas_call(
        functools.partial(_conv_kernel, tr=tr),
        grid=(B, nt),
        in_specs=[
            pl.BlockSpec((tr, C), lambda b, i: (b * nt + i, 0)),
            pl.BlockSpec((HALO, C), lambda b, i: (b * nt * per + jnp.maximum(i * per - 1, 0), 0)),
            pl.BlockSpec((None, HALO, C), lambda b, i: (b, 0, 0)),
            pl.BlockSpec((HALO, SUBLANES, C), lambda b, i: (0, 0, 0)),
            vec, vec, vec,
        ],
        out_specs=pl.BlockSpec((tr, C), lambda b, i: (b * nt + i, 0)),
        out_shape=jax.ShapeDtypeStruct((B * T, C), BF16),
        scratch_shapes=[pltpu.VMEM((HALO + tr + SUBLANES, C), F32),
                        pltpu.VMEM((SUBLANES, tr + CONV_SPAN, C), F32),
                        pltpu.VMEM((tr + CONV_SPAN, C), F32)],
        compiler_params=_params(2),
        name="conv_module",
    )(glu, glu, init, w_rep, b_dw.reshape(1, C), ln_g.reshape(1, C), ln_b.reshape(1, C))


def _rope_tables(pos):
    half = HEAD_DIM // 2
    inv = jnp.exp(-math.log(ROPE_BASE) * jnp.arange(half, dtype=F32) / half)
    ang = pos.astype(F32)[:, None] * inv[None, :]
    cos, sin = jnp.cos(ang), jnp.sin(ang)
    return jnp.concatenate([cos, cos], axis=-1), jnp.concatenate([-sin, sin], axis=-1)


def _prepare_weights(norm_mix, norm_ffn, w_in, b_forget, q_norm_gain, k_norm_gain, ret_norm_gain, w_out,
                     w_pw1, b_pw1, w_dw, b_dw, conv_ln_gain, conv_ln_bias, w_pw2, b_pw2,
                     w_ffn_gate, w_ffn_up, w_ffn_down):
    W = WIDTH
    D = w_in.shape[1]
    f0 = 3 * W
    f1 = f0 + N_HEADS
    w_b = w_in[0, :, f1:].astype(BF16)
    return dict(
        norm_mix=norm_mix, norm_ffn=norm_ffn,
        w_q=_Rhs(w_in, 0, D, col=0), w_k=_Rhs(w_in, 0, D, col=W), w_v=_Rhs(w_in, 0, D, col=2 * W),
        w_f=_Rhs(w_in, 0, D, col=f0),
        b_f=jnp.pad(b_forget[0], (0, LANES - N_HEADS)).reshape(1, LANES),
        w_qkb=_Rhs(w_b, None, D, col=0), w_vgb=_Rhs(w_b, None, D, col=2 * W),
        rope_scale=jnp.concatenate([jnp.ones((1, W), F32), jnp.full((1, W), HEAD_DIM ** -0.5, F32)], axis=1),
        gq=q_norm_gain[0].reshape(1, HEAD_DIM), gk=k_norm_gain[0].reshape(1, HEAD_DIM), gn=ret_norm_gain[0],
        w_oa=_Rhs(w_out, 0, W, k_blk=0), w_ob=_Rhs(w_out, 0, W, k_blk=1),
        w_1a=_Rhs(w_pw1, 0, D, col=0), w_1g=_Rhs(w_pw1, 0, D, col=w_pw1.shape[2] // 2),
        b_1a=b_pw1[0, :2 * W].reshape(1, -1), b_1g=b_pw1[0, 2 * W:].reshape(1, -1),
        w_dw=w_dw[0], b_dw=b_dw[0], ln_g=conv_ln_gain[0], ln_b=conv_ln_bias[0],
        w_2=_Rhs(w_pw2, 0, w_pw2.shape[1]), b_2=b_pw2[0].reshape(1, -1),
        w_g=[_Rhs(w_ffn_gate, l, D) for l in range(2)], w_u=[_Rhs(w_ffn_up, l, D) for l in range(2)],
        w_d=[_Rhs(w_ffn_down, l, w_ffn_down.shape[1]) for l in range(2)],
        d_ff=w_ffn_gate.shape[2],
    )


def _ffn(x, P, layer, tag):
    D = x.shape[1]
    h = _rmsnorm(x, P["norm_ffn"][layer], f"rms_ffn{layer}_{tag}")
    act, = _matmul(f"ffn_up{layer}_{tag}", [h], [P["w_g"][layer], P["w_u"][layer]], P["d_ff"],
                   [[(0, 0)], [(0, 1)]], _ep_swiglu, [BF16], tm=1024)
    out, = _matmul(f"ffn_down{layer}_{tag}", [act], [P["w_d"][layer]], D, [[(0, 0)]], _ep_residual, [F32],
                   tile=[x])
    return out


def _trunk(x3, pos, past, conv_past, P, tag):
    B, T, D = x3.shape
    M = B * T
    H = N_HEADS
    W = WIDTH
    x = x3.reshape(M, D)
    one = [[(0, 0)]]
    tall = dict(tm=1024, slabs=EPILOGUE_SLABS)
    wide = dict(tm=1024, tn=1024)
    busy = dict(wide, slabs=EPILOGUE_SLABS)

    h = _rmsnorm(x, P["norm_mix"][0], f"rms_mix0_{tag}")
    q, = _matmul(f"proj_q_{tag}", [h], [P["w_q"]], W, one,
                 functools.partial(_ep_headnorm, scales=(HEAD_DIM ** -0.5 * LOG2E,)), [BF16], full=[P["gq"]], **busy)
    k32, k16 = _matmul(f"proj_k_{tag}", [h], [P["w_k"]], W, one,
                       functools.partial(_ep_headnorm, scales=(1.0, 1.0)), [F32, BF16], full=[P["gk"]], **busy)
    v32, v16 = _matmul(f"proj_v_{tag}", [h], [P["w_v"]], W, one, _ep_identity, [F32, BF16], **wide)
    logf_pad, = _matmul(f"proj_f_{tag}", [h], [P["w_f"]], LANES, one, _ep_logsigmoid, [F32], col=[P["b_f"]],
                        tn=LANES)
    cos, sin = _rope_tables(pos)
    reps = max(1, wide["tm"] // T)
    cos, sin = jnp.tile(cos, (reps, 1)), jnp.tile(sin, (reps, 1))
    qk_b, = _matmul(f"proj_qkb_{tag}", [h], [P["w_qkb"]], 2 * W, one, _ep_rope, [BF16], col=[P["rope_scale"]],
                    row=[cos, sin], **busy)
    vg_b, = _matmul(f"proj_vgb_{tag}", [h], [P["w_vgb"]], 2 * W, one, _ep_identity, [F32], **wide)

    logf = logf_pad[:, :H].reshape(B, T, H)
    lf_t = logf.transpose(0, 2, 1).reshape(B * H, T)
    if past is None:
        _, c2_pieces = _cumsum_rows(lf_t, f"cumsum_{tag}", LOG2E)
        oa = _fox_prompt(q, k16, v16, c2_pieces, B, T)
        s0 = jnp.zeros((B, H, HEAD_DIM, HEAD_DIM), F32)
    else:
        kc, vc, lfc, s0 = past
        Pl = kc.shape[2]
        lf_all = jnp.concatenate([lfc.transpose(0, 2, 1).reshape(B * H, Pl), lf_t], axis=1)
        pad = (-lf_all.shape[1]) % LANES
        c2, _ = _cumsum_rows(jnp.pad(lf_all, ((0, 0), (0, pad))), f"cumsum_{tag}", LOG2E)
        oa = _fox_sample(q, k16, v16, kc, vc, c2[:, :Pl + T], B, T)
    ob, s_new = _retention(qk_b, vg_b, P["gn"], s0, B, T)
    x, = _matmul(f"out_proj_{tag}", [oa, ob], [P["w_oa"], P["w_ob"]], D, [[(0, 0), (1, 1)]], _ep_residual, [F32],
                 tile=[x], **wide)
    x = _ffn(x, P, 0, tag)

    h = _rmsnorm(x, P["norm_mix"][1], f"rms_mix1_{tag}")
    glu, = _matmul(f"conv_pw1_{tag}", [h], [P["w_1a"], P["w_1g"]], D, [[(0, 0)], [(0, 1)]], _ep_glu, [F32],
                   col=[P["b_1a"], P["b_1g"]], **tall)
    keep = CONV_WIDTH - 1
    assert T >= keep
    if conv_past is None:
        init = jnp.zeros((B, HALO, D), F32)
    else:
        init = jnp.pad(conv_past, ((0, 0), (HALO - keep, 0), (0, 0)))
    z = _conv_module(glu, init, P["w_dw"], P["b_dw"], P["ln_g"], P["ln_b"], B, T, tr=min(256, T))
    conv_new = glu.reshape(B, T, D)[:, -keep:]
    x, = _matmul(f"conv_pw2_{tag}", [z], [P["w_2"]], D, one, _ep_bias_residual, [F32], tile=[x], col=[P["b_2"]],
                 **wide)
    x = _ffn(x, P, 1, tag)

    return (x.reshape(B, T, D), k32.reshape(1, B, T, H, HEAD_DIM), v32.reshape(1, B, T, H, HEAD_DIM),
            logf[None], s_new[None], conv_new[None])


def kernel(x_prompt, x_sample, cache_k, cache_v, cache_logf, state_ret, state_conv,
           norm_mix, norm_ffn, w_in, b_forget, q_norm_gain, k_norm_gain, ret_norm_gain, w_out,
           w_pw1, b_pw1, w_dw, b_dw, conv_ln_gain, conv_ln_bias, w_pw2, b_pw2,
           w_ffn_gate, w_ffn_up, w_ffn_down):
    P = _prepare_weights(norm_mix, norm_ffn, w_in, b_forget, q_norm_gain, k_norm_gain, ret_norm_gain, w_out,
                         w_pw1, b_pw1, w_dw, b_dw, conv_ln_gain, conv_ln_bias, w_pw2, b_pw2,
                         w_ffn_gate, w_ffn_up, w_ffn_down)
    pos_p = jnp.arange(x_prompt.shape[1])
    y_p, k_p, v_p, lf_p, r_p, c_p = _trunk(x_prompt, pos_p, None, None, P, "p")
    _, Bs, Pl, H, hd = cache_k.shape
    pos_s = Pl + jnp.arange(x_sample.shape[1])
    past = (cache_k, cache_v, cache_logf.reshape(Bs, Pl, H), state_ret.reshape(Bs, H, hd, hd))
    conv_past = state_conv.reshape(state_conv.shape[1:])
    y_s, k_s, v_s, lf_s, r_s, c_s = _trunk(x_sample, pos_s, past, conv_past, P, "s")
    return (y_p, y_s, k_p, v_p, lf_p, r_p, c_p, k_s, v_s, lf_s, r_s, c_s)
```
